```python
import jax
import jax.numpy as jnp
from jax import lax
import numpy as np

D_MODEL = 1024
BATCH = 8
SEQ = 4096
DEPTH = 1
DEC_BATCH = 32
DEC_SEQ = 1
PAST_LEN = 16384
PAGE_SIZE = 128

HEAD_DIM = 64
FOX_HEADS = 8
MOBA_HEADS = 8
FOX_WIDTH = FOX_HEADS * HEAD_DIM
MOBA_WIDTH = MOBA_HEADS * HEAD_DIM
MOBA_BLOCK = 256
MOBA_TOPK = 3
Q_BLOCK = 128
D_FF = 2816
CONV_W = 3
ROPE_THETA = 10000.0
FORGET_BIAS = 3.0
RMS_EPS = 1e-6
NEG_INF = -1e30
ATTN_SCALE = HEAD_DIM ** -0.5
IN_SIZES = (FOX_WIDTH, FOX_WIDTH, FOX_WIDTH, FOX_HEADS, MOBA_WIDTH, MOBA_WIDTH, MOBA_WIDTH, D_MODEL, D_MODEL)
IN_COLS = 3 * FOX_WIDTH + FOX_HEADS + 3 * MOBA_WIDTH + 2 * D_MODEL

kernel_name = "fox_moba_convffn_hybrid_step"


def rms_norm(x, g):
    xf = x.astype(jnp.float32)
    y = xf * lax.rsqrt(jnp.mean(xf * xf, axis=-1, keepdims=True) + RMS_EPS)
    return (y * g.astype(jnp.float32)).astype(x.dtype)


def rope(x, pos):
    half = HEAD_DIM // 2
    inv = ROPE_THETA ** (-jnp.arange(half, dtype=jnp.float32) / half)
    ang = pos.astype(jnp.float32)[:, None] * inv[None, :]
    cos = jnp.cos(ang)[None, :, None, :]
    sin = jnp.sin(ang)[None, :, None, :]
    xf = x.astype(jnp.float32)
    x1, x2 = xf[..., :half], xf[..., half:]
    return jnp.concatenate([x1 * cos - x2 * sin, x2 * cos + x1 * sin], axis=-1).astype(x.dtype)


def branch_inputs(h, w_in, b_forget):
    b, s, _ = h.shape
    offs, acc = [], 0
    for n in IN_SIZES[:-1]:
        acc += n
        offs.append(acc)
    qa, ka, va, fa, qb, kb, vb, ga, gb = jnp.split(h @ w_in, offs, axis=-1)
    heads = lambda t, nh: t.reshape(b, s, nh, HEAD_DIM)
    logf = jax.nn.log_sigmoid(fa.astype(jnp.float32) + b_forget.astype(jnp.float32))
    return (heads(qa, FOX_HEADS), heads(ka, FOX_HEADS), heads(va, FOX_HEADS), logf,
            heads(qb, MOBA_HEADS), heads(kb, MOBA_HEADS), heads(vb, MOBA_HEADS), ga, gb)


def fox_attend(q, cq, qpos, k, v, ck, kpos):
    s = jnp.einsum('bqhd,bkhd->bhqk', q, k).astype(jnp.float32) * ATTN_SCALE
    s = s + cq[..., :, None] - ck[..., None, :]
    s = jnp.where(kpos[None, None, None, :] <= qpos[None, None, :, None], s, NEG_INF)
    p = jax.nn.softmax(s, axis=-1)
    return jnp.einsum('bhqk,bkhd->bqhd', p.astype(v.dtype), v)


def fox_prompt(q, k, v, logf):
    b, s, h, d = q.shape
    nqb = s // Q_BLOCK
    ct = jnp.cumsum(logf, axis=1).transpose(0, 2, 1)
    q_blocks = q.reshape(b, nqb, Q_BLOCK, h, d).transpose(1, 0, 2, 3, 4)
    c_blocks = ct.reshape(b, h, nqb, Q_BLOCK).transpose(2, 0, 1, 3)
    kpos = jnp.arange(s)

    def one_block(args):
        i, qi, ci = args
        return fox_attend(qi, ci, i * Q_BLOCK + jnp.arange(Q_BLOCK), k, v, ct, kpos)

    o = lax.map(one_block, (jnp.arange(nqb), q_blocks, c_blocks))
    return o.transpose(1, 0, 2, 3, 4).reshape(b, s, h, d)


def moba_blocks(k_parts, v_parts):
    b, _, h, d = k_parts[0].shape
    length = sum(p.shape[1] for p in k_parts)
    nb = max(-(-length // MOBA_BLOCK), MOBA_TOPK)
    pad = nb * MOBA_BLOCK - length
    kb = jnp.concatenate([*k_parts, jnp.zeros((b, pad, h, d), k_parts[-1].dtype)], axis=1)
    vb = jnp.concatenate([*v_parts, jnp.zeros((b, pad, h, d), v_parts[-1].dtype)], axis=1)
    kb = kb.reshape(b, nb, MOBA_BLOCK, h, d)
    vb = vb.reshape(b, nb, MOBA_BLOCK, h, d)
    kmean = kb.astype(jnp.float32).mean(axis=2)
    return kb, vb, kmean


def moba_attend(q, qpos, own, kb, vb, kmean):
    nh, nq = q.shape[0], q.shape[1]
    nb = kb.shape[0]
    gate = jnp.einsum('hqd,nhd->hqn', q.astype(jnp.float32), kmean)
    gate = jnp.where((jnp.arange(nb) < own)[None, None, :], gate, NEG_INF)
    _, sel = lax.top_k(gate, MOBA_TOPK)
    valid = sel < own
    hidx = jnp.arange(nh)[:, None, None]
    k_sel = kb[sel, :, hidx]
    v_sel = vb[sel, :, hidx]
    k_own = lax.dynamic_index_in_dim(kb, own, axis=0, keepdims=False)
    v_own = lax.dynamic_index_in_dim(vb, own, axis=0, keepdims=False)
    s_sel = jnp.einsum('hqd,hqnkd->hqnk', q, k_sel).astype(jnp.float32) * ATTN_SCALE
    s_sel = jnp.where(valid[..., None], s_sel, NEG_INF)
    s_own = jnp.einsum('hqd,khd->hqk', q, k_own).astype(jnp.float32) * ATTN_SCALE
    own_pos = own * MOBA_BLOCK + jnp.arange(MOBA_BLOCK)
    s_own = jnp.where(own_pos[None, None, :] <= qpos[None, :, None], s_own, NEG_INF)
    n_sel = MOBA_TOPK * MOBA_BLOCK
    s = jnp.concatenate([s_sel.reshape(nh, nq, n_sel), s_own], axis=-1)
    p = jax.nn.softmax(s, axis=-1).astype(v_own.dtype)
    p_sel = p[..., :n_sel].reshape(nh, nq, MOBA_TOPK, MOBA_BLOCK)
    p_own = p[..., n_sel:]
    return jnp.einsum('hqnk,hqnkd->hqd', p_sel, v_sel) + jnp.einsum('hqk,khd->hqd', p_own, v_own)


def moba_prompt(q, k, v):
    b, s, h, d = q.shape
    kb, vb, kmean = moba_blocks([k], [v])
    nqb = s // Q_BLOCK
    qs = q.reshape(b, nqb, Q_BLOCK, h, d).transpose(0, 1, 3, 2, 4).reshape(b * nqb, h, Q_BLOCK, d)

    def one_block(args):
        idx, qi = args
        bi, i = idx // nqb, idx % nqb
        start = i * Q_BLOCK
        return moba_attend(qi, start + jnp.arange(Q_BLOCK), start // MOBA_BLOCK, kb[bi], vb[bi], kmean[bi])

    o = lax.map(one_block, (jnp.arange(b * nqb), qs))
    return o.reshape(b, nqb, h, Q_BLOCK, d).transpose(0, 1, 3, 2, 4).reshape(b, s, h, d)


def moba_sample(q, k_parts, v_parts, past):
    b, t, h, d = q.shape
    kb, vb, kmean = moba_blocks(k_parts, v_parts)
    qpos = past + jnp.arange(t)
    per_query = jax.vmap(moba_attend, in_axes=(0, 0, 0, None, None, None))
    per_seq = jax.vmap(per_query, in_axes=(0, None, None, 0, 0, 0))
    o = per_seq(q[:, :, :, None, :], qpos[:, None], qpos // MOBA_BLOCK, kb, vb, kmean)
    return o[:, :, :, 0, :]


def merge_branches(oa, ob, ga, gb, w_fox_o, w_moba_o, w_out):
    b, s = oa.shape[0], oa.shape[1]
    pa = oa.reshape(b, s, FOX_WIDTH) @ w_fox_o
    pb = ob.reshape(b, s, MOBA_WIDTH) @ w_moba_o
    return (jax.nn.sigmoid(ga) * pa + jax.nn.sigmoid(gb) * pb) @ w_out


def conv_ffn(h, prev, w_up, conv_w, conv_b, w_down):
    u = h @ w_up
    s = u.shape[1]
    full = jnp.concatenate([prev.astype(u.dtype), u], axis=1)
    c = conv_b + conv_w[0] * full[:, 0:s]
    for i in range(1, CONV_W):
        c = c + conv_w[i] * full[:, i:i + s]
    a, g = jnp.split(c, 2, axis=-1)
    y = (jax.nn.gelu(a, approximate=False) * g) @ w_down
    return y, full[:, s:]


def gather_pages(pool, layer, page_table):
    rows = pool[layer, page_table]
    return rows.reshape(page_table.shape[0], -1, *pool.shape[3:])


def prompt_layer(x, g_mix, w_in, b_forget, w_fox_o, w_moba_o, w_out, g_ffn, w_up, conv_w, conv_b, w_down):
    b, s, _ = x.shape
    pos = jnp.arange(s)
    h = rms_norm(x, g_mix)
    qa, ka, va, logf, qb, kb, vb, ga, gb = branch_inputs(h, w_in, b_forget)
    qb, kb = rope(qb, pos), rope(kb, pos)
    oa = fox_prompt(qa, ka, va, logf)
    ob = moba_prompt(qb, kb, vb)
    x = x + merge_branches(oa, ob, ga, gb, w_fox_o, w_moba_o, w_out)
    prev = jnp.zeros((b, CONV_W - 1, 2 * D_FF), x.dtype)
    f, conv_tail = conv_ffn(rms_norm(x, g_ffn), prev, w_up, conv_w, conv_b, w_down)
    return x + f, (ka, va, logf, kb, vb, conv_tail)


def sample_layer(x, fox_k_pool, fox_v_pool, fox_logf_pool, moba_k_pool, moba_v_pool, conv_prev, page_table, layer,
                 g_mix, w_in, b_forget, w_fox_o, w_moba_o, w_out, g_ffn, w_up, conv_w, conv_b, w_down):
    t = x.shape[1]
    past = page_table.shape[1] * fox_k_pool.shape[2]
    pos = past + jnp.arange(t)
    h = rms_norm(x, g_mix)
    qa, ka, va, logf, qb, kb, vb, ga, gb = branch_inputs(h, w_in, b_forget)
    qb, kb = rope(qb, pos), rope(kb, pos)
    k_all = jnp.concatenate([gather_pages(fox_k_pool, layer, page_table), ka], axis=1)
    v_all = jnp.concatenate([gather_pages(fox_v_pool, layer, page_table), va], axis=1)
    logf_all = jnp.concatenate([gather_pages(fox_logf_pool, layer, page_table).astype(jnp.float32), logf], axis=1)
    ct = jnp.cumsum(logf_all, axis=1).transpose(0, 2, 1)
    oa = fox_attend(qa, ct[:, :, past:], pos, k_all, v_all, ct, jnp.arange(past + t))
    ob = moba_sample(qb, [gather_pages(moba_k_pool, layer, page_table), kb],
                     [gather_pages(moba_v_pool, layer, page_table), vb], past)
    x = x + merge_branches(oa, ob, ga, gb, w_fox_o, w_moba_o, w_out)
    f, conv_tail = conv_ffn(rms_norm(x, g_ffn), conv_prev, w_up, conv_w, conv_b, w_down)
    return x + f, (ka, va, logf, kb, vb, conv_tail)


def setup_inputs(seed: int = 0) -> dict:
    key = jax.random.key(seed)
    ks = jax.random.split(key, 24)
    f32 = jnp.float32
    n_pages = PAST_LEN // PAGE_SIZE
    n_pool = (DEC_BATCH * n_pages * 5) // 4
    nrm = lambda k, shape, scale=1.0: scale * jax.random.normal(k, shape, f32)
    page_table = jax.random.permutation(ks[0], n_pool)[: DEC_BATCH * n_pages].reshape(DEC_BATCH, n_pages).astype(jnp.int32)
    return {
        "x_prompt": nrm(ks[1], (BATCH, SEQ, D_MODEL)),
        "x_sample": nrm(ks[2], (DEC_BATCH, DEC_SEQ, D_MODEL)),
        "cache_fox_k": nrm(ks[3], (DEPTH, n_pool, PAGE_SIZE, FOX_HEADS, HEAD_DIM)),
        "cache_fox_v": nrm(ks[4], (DEPTH, n_pool, PAGE_SIZE, FOX_HEADS, HEAD_DIM)),
        "cache_fox_logf": jax.nn.log_sigmoid(FORGET_BIAS + nrm(ks[5], (DEPTH, n_pool, PAGE_SIZE, FOX_HEADS))),
        "cache_moba_k": nrm(ks[6], (DEPTH, n_pool, PAGE_SIZE, MOBA_HEADS, HEAD_DIM)),
        "cache_moba_v": nrm(ks[7], (DEPTH, n_pool, PAGE_SIZE, MOBA_HEADS, HEAD_DIM)),
        "state_ffn_conv": nrm(ks[8], (DEPTH, DEC_BATCH, CONV_W - 1, 2 * D_FF)),
        "page_table": page_table,
        "norm_mix_g": 1.0 + nrm(ks[9], (DEPTH, D_MODEL), 0.02),
        "w_in": nrm(ks[10], (DEPTH, D_MODEL, IN_COLS), D_MODEL ** -0.5),
        "b_forget": FORGET_BIAS + nrm(ks[11], (DEPTH, FOX_HEADS), 0.1),
        "w_fox_o": nrm(ks[12], (DEPTH, FOX_WIDTH, D_MODEL), FOX_WIDTH ** -0.5),
        "w_moba_o": nrm(ks[13], (DEPTH, MOBA_WIDTH, D_MODEL), MOBA_WIDTH ** -0.5),
        "w_out": nrm(ks[14], (DEPTH, D_MODEL, D_MODEL), D_MODEL ** -0.5),
        "norm_ffn_g": 1.0 + nrm(ks[15], (DEPTH, D_MODEL), 0.02),
        "w_up": nrm(ks[16], (DEPTH, D_MODEL, 2 * D_FF), D_MODEL ** -0.5),
        "conv_w": nrm(ks[17], (DEPTH, CONV_W, 2 * D_FF), CONV_W ** -0.5),
        "conv_b": nrm(ks[18], (DEPTH, 2 * D_FF), 0.01),
        "w_down": nrm(ks[19], (DEPTH, D_FF, D_MODEL), D_FF ** -0.5),
        "norm_final_g": 1.0 + nrm(ks[20], (D_MODEL,), 0.02),
    }


def reference(x_prompt, x_sample, cache_fox_k, cache_fox_v, cache_fox_logf, cache_moba_k, cache_moba_v,
              state_ffn_conv, page_table, norm_mix_g, w_in, b_forget, w_fox_o, w_moba_o, w_out,
              norm_ffn_g, w_up, conv_w, conv_b, w_down, norm_final_g):
    xp, xs = x_prompt, x_sample
    p_new, s_new = [], []
    for l in range(DEPTH):
        w = (norm_mix_g[l], w_in[l], b_forget[l], w_fox_o[l], w_moba_o[l], w_out[l],
             norm_ffn_g[l], w_up[l], conv_w[l], conv_b[l], w_down[l])
        xp, st_p = prompt_layer(xp, *w)
        xs, st_s = sample_layer(xs, cache_fox_k, cache_fox_v, cache_fox_logf, cache_moba_k, cache_moba_v,
                                state_ffn_conv[l], page_table, l, *w)
        p_new.append(st_p)
        s_new.append(st_s)
    y_prompt = rms_norm(xp, norm_final_g)
    y_sample = rms_norm(xs, norm_final_g)
    p_fk, p_fv, p_fl, p_mk, p_mv, p_cv = [jnp.stack(z) for z in zip(*p_new)]
    s_fk, s_fv, s_fl, s_mk, s_mv, s_cv = [jnp.stack(z) for z in zip(*s_new)]
    return (y_prompt, y_sample, p_fk, p_fv, p_fl, p_mk, p_mv, p_cv, s_fk, s_fv, s_fl, s_mk, s_mv, s_cv)
```

```python
import functools

import jax
import jax.numpy as jnp
from jax import lax
from jax.experimental import pallas as pl
from jax.experimental.pallas import tpu as pltpu

HEAD_DIM = 64
HALF = HEAD_DIM // 2
LANES = 128
MOBA_BLOCK = 256
MOBA_TOPK = 3
CONV_W = 3
ROPE_THETA = 10000.0
RMS_EPS = 1e-6
NEG_INF = -1e30
MASK_BIAS = -2e30
ATTN_SCALE = HEAD_DIM ** -0.5
VMEM_LIMIT = 56 * 1024 * 1024

F32 = jnp.float32
BF16 = jnp.bfloat16
HIGHEST = lax.Precision.HIGHEST


def _nt(a, b, precision=None):
    return lax.dot_general(a, b, (((1,), (1,)), ((), ())), preferred_element_type=F32, precision=precision)


def _nn(a, b, precision=None):
    return jnp.dot(a, b, preferred_element_type=F32, precision=precision)


def _params(sem):
    return pltpu.CompilerParams(dimension_semantics=sem, vmem_limit_bytes=VMEM_LIMIT)


def _resident(shape):
    nd = len(shape)
    return pl.BlockSpec(shape, lambda *_: (0,) * nd, pipeline_mode=pl.Buffered(1))


def _inproj_kernel(x_ref, g_ref, w_ref, bf_ref, cos_ref, sina_ref, sinb_ref,
                   qa_ref, ka_ref, va_ref, qb_ref, kb_ref, vb_ref, ga_ref, gb_ref, lf_ref, *rest,
                   aw, d_model, emit_lft):
    x = x_ref[...]
    h = (x * lax.rsqrt(jnp.mean(x * x, axis=-1, keepdims=True) + RMS_EPS) * g_ref[...]).astype(BF16)

    def proj(c0, width):
        return _nn(h, w_ref[:, c0:c0 + width])

    qa_ref[...] = proj(0, aw)
    ka_ref[...] = proj(aw, aw)
    va_ref[...] = proj(2 * aw, aw)

    cos, sina, sinb = cos_ref[...], sina_ref[...], sinb_ref[...]

    def rope_store(t, o_ref):
        for s in range(aw // LANES):
            xs = t[:, s * LANES:(s + 1) * LANES]
            o_ref[:, s * LANES:(s + 1) * LANES] = (
                xs * cos + pltpu.roll(xs, LANES - HALF, 1) * sina + pltpu.roll(xs, HALF, 1) * sinb)

    rope_store(proj(3 * aw, aw), qb_ref)
    rope_store(proj(4 * aw, aw), kb_ref)
    vb_ref[...] = proj(5 * aw, aw)
    ga_ref[...] = jax.nn.sigmoid(proj(6 * aw, d_model))
    gb_ref[...] = jax.nn.sigmoid(proj(6 * aw + d_model, d_model))

    f = proj(6 * aw + 2 * d_model, LANES) + bf_ref[...]
    lsig = -(jnp.maximum(-f, 0.0) + jnp.log1p(jnp.exp(-jnp.abs(f))))
    nh = lf_ref.shape[-1]
    lf_ref[...] = lsig[:, :nh]
    if emit_lft:
        rest[0][...] = lsig.T[:nh, :]


def _inproj(x, g, w, bf, cos, sina, sinb, *, tm, aw, nh, emit_lft):
    n, d = x.shape
    nt = n // tm
    tb = cos.shape[0] // tm
    row = lambda i: (i, 0)
    tbl = lambda i: (i % tb, 0)
    out_shape = [jax.ShapeDtypeStruct((n, aw), F32)] * 6 + [jax.ShapeDtypeStruct((n, d), F32)] * 2 + [
        jax.ShapeDtypeStruct((n, nh), F32)]
    out_specs = [pl.BlockSpec((tm, aw), row)] * 6 + [pl.BlockSpec((tm, d), row)] * 2 + [pl.BlockSpec((tm, nh), row)]
    if emit_lft:
        out_shape.append(jax.ShapeDtypeStruct((nh, n), F32))
        out_specs.append(pl.BlockSpec((nh, tm), lambda i: (0, i)))
    return pl.pallas_call(
        functools.partial(_inproj_kernel, aw=aw, d_model=d, emit_lft=emit_lft),
        grid=(nt,),
        in_specs=[pl.BlockSpec((tm, d), row), _resident(g.shape), _resident(w.shape), _resident(bf.shape),
                  pl.BlockSpec((tm, LANES), tbl), pl.BlockSpec((tm, LANES), tbl), pl.BlockSpec((tm, LANES), tbl)],
        out_specs=out_specs,
        out_shape=out_shape,
        compiler_params=_params(("arbitrary",)),
        name="inproj",
    )(x, g, w, bf, cos, sina, sinb)


def _softmax_step(s, v16, m_ref, l_ref, acc_ref):
    m_old = m_ref[...]
    m_new = jnp.maximum(m_old, jnp.max(s, axis=1, keepdims=True))
    alpha = jnp.exp(m_old - m_new)
    p = jnp.exp(s - m_new)
    l_ref[...] = alpha * l_ref[...] + jnp.sum(p, axis=1, keepdims=True)
    acc_ref[...] = alpha * acc_ref[...] + _nn(p.astype(BF16), v16)
    m_ref[...] = m_new


def _attn_init(m_ref, l_ref, acc_ref):
    m_ref[...] = jnp.full(m_ref.shape, NEG_INF, F32)
    l_ref[...] = jnp.zeros(l_ref.shape, F32)
    acc_ref[...] = jnp.zeros(acc_ref.shape, F32)


def _causal(s, t):
    qpos = lax.broadcasted_iota(jnp.int32, (t, t), 0)
    kpos = lax.broadcasted_iota(jnp.int32, (t, t), 1)
    return jnp.where(kpos <= qpos, s, NEG_INF)


def _fox_prompt_kernel(q_ref, k_ref, v_ref, lft_ref, o_ref, k16, v16, negc, m_ref, l_ref, acc_ref, *, t, seq):
    i = pl.program_id(2)

    @pl.when(i == 0)
    def _():
        k16[...] = k_ref[...].astype(BF16)
        v16[...] = v_ref[...].astype(BF16)
        c = lft_ref[0]
        lane = lax.broadcasted_iota(jnp.int32, c.shape, 1)
        sh = 1
        while sh < seq:
            c = c + jnp.where(lane >= sh, pltpu.roll(c, sh, 1), 0.0)
            sh *= 2
        negc[...] = -c

    q = q_ref[...]
    lane = lax.broadcasted_iota(jnp.int32, q.shape, 1)
    qh = (jnp.where(lane < HEAD_DIM, q, 0.0).astype(BF16), jnp.where(lane >= HEAD_DIM, q, 0.0).astype(BF16))
    _attn_init(m_ref, l_ref, acc_ref)

    def block(j, masked):
        off = pl.multiple_of(j * t, t)
        kj = k16[pl.ds(off, t), :]
        vj = v16[pl.ds(off, t), :]
        for e in range(2):
            s = _nt(qh[e], kj) + negc[e:e + 1, pl.ds(off, t)]
            if masked:
                s = _causal(s, t)
            _softmax_step(s, vj, m_ref.at[e], l_ref.at[e], acc_ref.at[e])

    def body(j, carry):
        block(j, False)
        return carry

    lax.fori_loop(0, i, body, 0)
    block(i, True)
    o_ref[...] = jnp.where(lane < HEAD_DIM, acc_ref[0] / l_ref[0], acc_ref[1] / l_ref[1])


def _fox_prompt(q, k, v, lft, *, batch, seq, t):
    n, aw = q.shape
    hp = aw // LANES
    nq = seq // t
    lft3 = lft.reshape(hp, 2, n)
    qmap = lambda b, h, i: (b * nq + i, h)
    kmap = lambda b, h, i: (b, h)
    return pl.pallas_call(
        functools.partial(_fox_prompt_kernel, t=t, seq=seq),
        grid=(batch, hp, nq),
        in_specs=[pl.BlockSpec((t, LANES), qmap), pl.BlockSpec((seq, LANES), kmap), pl.BlockSpec((seq, LANES), kmap),
                  pl.BlockSpec((1, 2, seq), lambda b, h, i: (h, 0, b))],
        out_specs=pl.BlockSpec((t, LANES), qmap),
        out_shape=jax.ShapeDtypeStruct((n, aw), F32),
        scratch_shapes=[pltpu.VMEM((seq, LANES), BF16), pltpu.VMEM((seq, LANES), BF16), pltpu.VMEM((2, seq), F32),
                        pltpu.VMEM((2, t, 1), F32), pltpu.VMEM((2, t, 1), F32), pltpu.VMEM((2, t, LANES), F32)],
        compiler_params=_params(("arbitrary", "arbitrary", "arbitrary")),
        name="fox_prompt",
    )(q, k, v, lft3)


def _top_blocks(g, n, limit):
    nf = n.astype(F32)
    g = jnp.where((n >= 0) & (n < limit), g, NEG_INF)
    sel = jnp.zeros(g.shape, F32)
    for _ in range(MOBA_TOPK):
        m = jnp.max(g, axis=1, keepdims=True)
        idx = jnp.min(jnp.where(g == m, nf, 1e9), axis=1, keepdims=True)
        hit = nf == idx
        sel = jnp.where(hit & (idx < limit) & (idx >= 0), 1.0, sel)
        g = jnp.where(hit, -3e38, g)
    return sel > 0.5


def _moba_prompt_kernel(q_ref, k_ref, v_ref, o_ref, ka16, kb16, v16, kma, kmb, m_ref, l_ref, acc_ref, *, seq):
    t = MOBA_BLOCK
    nb = seq // t
    i = pl.program_id(2)
    lane_k = lax.broadcasted_iota(jnp.int32, (seq, LANES), 1)
    blk_k = lax.broadcasted_iota(jnp.int32, (seq, LANES), 0) // t

    @pl.when(i == 0)
    def _():
        k = k_ref[...]
        ka16[...] = jnp.where(lane_k < HEAD_DIM, k, jnp.where(lane_k - HEAD_DIM == blk_k, 1.0, 0.0)).astype(BF16)
        kb16[...] = jnp.where(lane_k >= HEAD_DIM, k, jnp.where(lane_k == blk_k, 1.0, 0.0)).astype(BF16)
        v16[...] = v_ref[...].astype(BF16)
        kma[...] = jnp.zeros(kma.shape, F32)
        kmb[...] = jnp.zeros(kmb.shape, F32)
        for n in range(nb):
            km = jnp.sum(k[n * t:(n + 1) * t, :], axis=0, keepdims=True) * (1.0 / t)
            kma[HEAD_DIM + n:HEAD_DIM + n + 1, :] = km
            kmb[n:n + 1, :] = km

    q = q_ref[...]
    lane = lax.broadcasted_iota(jnp.int32, q.shape, 1)
    in_a = lane < HEAD_DIM
    qh = []
    for e, (own, km, n) in enumerate(((in_a, kma, lane - HEAD_DIM), (~in_a, kmb, lane))):
        gate = _nt(jnp.where(own, q, 0.0), km[...], precision=HIGHEST)
        sel = _top_blocks(gate, n, i)
        bias = jnp.where((n >= 0) & (n < i) & ~sel, MASK_BIAS, 0.0)
        qh.append(jnp.where(own, q, bias).astype(BF16))
    ks = (ka16, kb16)
    _attn_init(m_ref, l_ref, acc_ref)

    def block(j, masked):
        off = pl.multiple_of(j * t, t)
        vj = v16[pl.ds(off, t), :]
        for e in range(2):
            s = _nt(qh[e], ks[e][pl.ds(off, t), :])
            if masked:
                s = _causal(s, t)
            _softmax_step(s, vj, m_ref.at[e], l_ref.at[e], acc_ref.at[e])

    def body(j, carry):
        block(j, False)
        return carry

    lax.fori_loop(0, i, body, 0)
    block(i, True)
    o_ref[...] = jnp.where(in_a, acc_ref[0] / l_ref[0], acc_ref[1] / l_ref[1])


def _moba_prompt(q, k, v, *, batch, seq):
    n, aw = q.shape
    t = MOBA_BLOCK
    hp = aw // LANES
    nq = seq // t
    qmap = lambda b, h, i: (b * nq + i, h)
    kmap = lambda b, h, i: (b, h)
    return pl.pallas_call(
        functools.partial(_moba_prompt_kernel, seq=seq),
        grid=(batch, hp, nq),
        in_specs=[pl.BlockSpec((t, LANES), qmap), pl.BlockSpec((seq, LANES), kmap), pl.BlockSpec((seq, LANES), kmap)],
        out_specs=pl.BlockSpec((t, LANES), qmap),
        out_shape=jax.ShapeDtypeStruct((n, aw), F32),
        scratch_shapes=[pltpu.VMEM((seq, LANES), BF16), pltpu.VMEM((seq, LANES), BF16), pltpu.VMEM((seq, LANES), BF16),
                        pltpu.VMEM((LANES, LANES), F32), pltpu.VMEM((LANES, LANES), F32),
                        pltpu.VMEM((2, t, 1), F32), pltpu.VMEM((2, t, 1), F32), pltpu.VMEM((2, t, LANES), F32)],
        compiler_params=_params(("arbitrary", "arbitrary", "arbitrary")),
        name="moba_prompt",
    )(q, k, v)


def _merge_kernel(oa_ref, ob_ref, ga_ref, gb_ref, x_ref, wfo_ref, wmo_ref, wout_ref, g_ref, x1_ref, h_ref):
    pa = _nn(oa_ref[...].astype(BF16), wfo_ref[...])
    pb = _nn(ob_ref[...].astype(BF16), wmo_ref[...])
    mix = ga_ref[...] * pa + gb_ref[...] * pb
    x1 = x_ref[...] + _nn(mix.astype(BF16), wout_ref[...])
    x1_ref[...] = x1
    h_ref[...] = (x1 * lax.rsqrt(jnp.mean(x1 * x1, axis=-1, keepdims=True) + RMS_EPS) * g_ref[...]).astype(BF16)


def _merge(oa, ob, ga, gb, x, wfo, wmo, wout, g, *, tm):
    n, d = x.shape
    aw = oa.shape[1]
    row = lambda i: (i, 0)
    return pl.pallas_call(
        _merge_kernel,
        grid=(n // tm,),
        in_specs=[pl.BlockSpec((tm, aw), row), pl.BlockSpec((tm, aw), row), pl.BlockSpec((tm, d), row),
                  pl.BlockSpec((tm, d), row), pl.BlockSpec((tm, d), row),
                  _resident(wfo.shape), _resident(wmo.shape), _resident(wout.shape), _resident(g.shape)],
        out_specs=[pl.BlockSpec((tm, d), row), pl.BlockSpec((tm, d), row)],
        out_shape=[jax.ShapeDtypeStruct((n, d), F32), jax.ShapeDtypeStruct((n, d), BF16)],
        compiler_params=_params(("arbitrary",)),
        name="merge",
    )(oa, ob, ga, gb, x, wfo, wmo, wout, g)


def _gated_act(ca, cg):
    gelu = 0.5 * ca * (1.0 + lax.erf(ca * (2.0 ** -0.5)))
    return (gelu * cg).astype(BF16)


def _ffn_prompt_kernel(h_ref, x1_ref, wa_ref, wg_ref, cwa_ref, cwg_ref, cba_ref, cbg_ref, wd_ref, gf_ref,
                       y_ref, ta_ref, tg_ref, acc_ref, prev_ref, *, tiles_per_seq, n_chunks):
    i = pl.program_id(0)
    c = pl.program_id(1)
    h = h_ref[...]
    tm = h.shape[0]
    seq_start = i % tiles_per_seq == 0
    row = lax.broadcasted_iota(jnp.int32, (8, wa_ref.shape[1]), 0)

    def conv(w_ref, cw_ref, cb_ref, slot, tail_ref):
        u = _nn(h, w_ref[...])
        last = u[tm - 8:, :]
        tail_ref[0] = last
        @pl.when(seq_start)
        def _():
            prev_ref[slot] = jnp.zeros(last.shape, F32)

        prev = prev_ref[slot]
        prev_ref[slot] = last
        s1 = pltpu.roll(u, 1, 0)
        s2 = pltpu.roll(u, 2, 0)
        s1 = jnp.concatenate([jnp.where(row < 1, pltpu.roll(prev, 1, 0), s1[:8]), s1[8:]], axis=0)
        s2 = jnp.concatenate([jnp.where(row < 2, pltpu.roll(prev, 2, 0), s2[:8]), s2[8:]], axis=0)
        cw = cw_ref[...]
        return cb_ref[...] + cw[0:1] * s2 + cw[1:2] * s1 + cw[2:3] * u

    ca = conv(wa_ref, cwa_ref, cba_ref, c, ta_ref)
    cg = conv(wg_ref, cwg_ref, cbg_ref, n_chunks + c, tg_ref)
    part = _nn(_gated_act(ca, cg), wd_ref[...])

    @pl.when(c == 0)
    def _():
        acc_ref[...] = part

    @pl.when(c > 0)
    def _():
        acc_ref[...] += part

    @pl.when(c == n_chunks - 1)
    def _():
        xo = x1_ref[...] + acc_ref[...]
        y_ref[...] = xo * lax.rsqrt(jnp.mean(xo * xo, axis=-1, keepdims=True) + RMS_EPS) * gf_ref[...]


def _ffn_prompt(h, x1, wup, cw, cb, wd, gf, *, batch, seq, tm, tf):
    n, d = x1.shape
    dff = wd.shape[0]
    nc = dff // tf
    tps = seq // tm
    row = lambda i, c: (i, 0)
    ca = lambda i, c: (0, c)
    cg = lambda i, c: (0, nc + c)
    tail = lambda i, c: (i, 0, c)
    return pl.pallas_call(
        functools.partial(_ffn_prompt_kernel, tiles_per_seq=tps, n_chunks=nc),
        grid=(n // tm, nc),
        in_specs=[pl.BlockSpec((tm, d), row), pl.BlockSpec((tm, d), row),
                  pl.BlockSpec((d, tf), ca), pl.BlockSpec((d, tf), cg),
                  pl.BlockSpec((CONV_W, tf), ca), pl.BlockSpec((CONV_W, tf), cg),
                  pl.BlockSpec((1, tf), ca), pl.BlockSpec((1, tf), cg),
                  pl.BlockSpec((tf, d), lambda i, c: (c, 0)), pl.BlockSpec((1, d), lambda i, c: (0, 0))],
        out_specs=[pl.BlockSpec((tm, d), row), pl.BlockSpec((1, 8, tf), tail), pl.BlockSpec((1, 8, tf), tail)],
        out_shape=[jax.ShapeDtypeStruct((n, d), F32), jax.ShapeDtypeStruct((n // tm, 8, dff), F32),
                   jax.ShapeDtypeStruct((n // tm, 8, dff), F32)],
        scratch_shapes=[pltpu.VMEM((tm, d), F32), pltpu.VMEM((2 * nc, 8, tf), F32)],
        compiler_params=_params(("arbitrary", "arbitrary")),
        name="ffn_prompt",
    )(h, x1, wup, wup, cw, cw, cb, cb, wd, gf)


def _ffn_sample_kernel(h_ref, x1_ref, p0a_ref, p0g_ref, p1a_ref, p1g_ref, wa_ref, wg_ref, cwa_ref, cwg_ref,
                       cba_ref, cbg_ref, wd_ref, gf_ref, y_ref, ua_ref, ug_ref, acc_ref, *, n_chunks):
    c = pl.program_id(0)
    h = h_ref[...]

    def conv(w_ref, cw_ref, cb_ref, p0_ref, p1_ref, u_ref):
        u = _nn(h, w_ref[...])
        u_ref[...] = u
        cw = cw_ref[...]
        return cb_ref[...] + cw[0:1] * p0_ref[...] + cw[1:2] * p1_ref[...] + cw[2:3] * u

    ca = conv(wa_ref, cwa_ref, cba_ref, p0a_ref, p1a_ref, ua_ref)
    cg = conv(wg_ref, cwg_ref, cbg_ref, p0g_ref, p1g_ref, ug_ref)
    part = _nn(_gated_act(ca, cg), wd_ref[...])

    @pl.when(c == 0)
    def _():
        acc_ref[...] = part

    @pl.when(c > 0)
    def _():
        acc_ref[...] += part

    @pl.when(c == n_chunks - 1)
    def _():
        xo = x1_ref[...] + acc_ref[...]
        y_ref[...] = xo * lax.rsqrt(jnp.mean(xo * xo, axis=-1, keepdims=True) + RMS_EPS) * gf_ref[...]


def _ffn_sample(h, x1, prev, wup, cw, cb, wd, gf, *, tf):
    n, d = x1.shape
    dff = wd.shape[0]
    nc = dff // tf
    full = lambda c: (0, 0)
    col = lambda k: (lambda c: (0, k * nc + c))
    return pl.pallas_call(
        functools.partial(_ffn_sample_kernel, n_chunks=nc),
        grid=(nc,),
        in_specs=[pl.BlockSpec((n, d), full), pl.BlockSpec((n, d), full),
                  pl.BlockSpec((n, tf), col(0)), pl.BlockSpec((n, tf), col(1)),
                  pl.BlockSpec((n, tf), col(2)), pl.BlockSpec((n, tf), col(3)),
                  pl.BlockSpec((d, tf), col(0)), pl.BlockSpec((d, tf), col(1)),
                  pl.BlockSpec((CONV_W, tf), col(0)), pl.BlockSpec((CONV_W, tf), col(1)),
                  pl.BlockSpec((1, tf), col(0)), pl.BlockSpec((1, tf), col(1)),
                  pl.BlockSpec((tf, d), lambda c: (c, 0)), pl.BlockSpec((1, d), full)],
        out_specs=[pl.BlockSpec((n, d), full), pl.BlockSpec((n, tf), col(0)), pl.BlockSpec((n, tf), col(0))],
        out_shape=[jax.ShapeDtypeStruct((n, d), F32), jax.ShapeDtypeStruct((n, dff), F32),
                   jax.ShapeDtypeStruct((n, dff), F32)],
        scratch_shapes=[pltpu.VMEM((n, d), F32)],
        compiler_params=_params(("arbitrary",)),
        name="ffn_sample",
    )(h, x1, prev, prev, prev, prev, wup, wup, cw, cw, cb, cb, wd, gf)


ROWS = 16


def _head_rows(q, nh):
    shape = (ROWS, q.shape[1])
    sub = lax.broadcasted_iota(jnp.int32, shape, 0)
    lane = lax.broadcasted_iota(jnp.int32, shape, 1)
    own = lane // HEAD_DIM == sub
    return jnp.where(own, jnp.broadcast_to(q, shape), 0.0), own


def _fox_decode_kernel(pt_ref, q_ref, kn_ref, vn_ref, lfn_ref, *refs, pg, nh):
    k_refs, v_refs, lf_refs = refs[:pg], refs[pg:2 * pg], refs[2 * pg:3 * pg]
    o_ref, m_ref, l_ref, acc_ref, carry_ref = refs[3 * pg:]
    j = pl.program_id(1)

    @pl.when(j == 0)
    def _():
        _attn_init(m_ref, l_ref, acc_ref)
        carry_ref[...] = jnp.zeros(carry_ref.shape, F32)

    qrows, own = _head_rows(q_ref[0], nh)
    q16 = qrows.astype(BF16)
    page = k_refs[0].shape[1]
    tri = (lax.broadcasted_iota(jnp.int32, (page, page), 0) <= lax.broadcasted_iota(jnp.int32, (page, page), 1))
    tri = tri.astype(F32)
    zpad = jnp.zeros((ROWS - nh, page), F32)

    for p in range(pg):
        lf = jnp.concatenate([lf_refs[p][0], zpad], axis=0)
        c = carry_ref[...] + _nn(lf, tri, precision=HIGHEST)
        carry_ref[...] = c[:, page - 1:page]
        s = _nt(q16, k_refs[p][0].astype(BF16)) - c
        _softmax_step(s, v_refs[p][0].astype(BF16), m_ref, l_ref, acc_ref)

    @pl.when(j == pl.num_programs(1) - 1)
    def _():
        lfn = jnp.concatenate([lfn_ref[0], jnp.zeros((ROWS - nh, 1), F32)], axis=0)
        s = jnp.sum(qrows * kn_ref[0], axis=1, keepdims=True) - (carry_ref[...] + lfn)
        m_old = m_ref[...]
        m_new = jnp.maximum(m_old, s)
        alpha = jnp.exp(m_old - m_new)
        p = jnp.exp(s - m_new)
        l = alpha * l_ref[...] + p
        acc = alpha * acc_ref[...] + p * vn_ref[0]
        o_ref[0] = jnp.sum(jnp.where(own, acc / l, 0.0), axis=0, keepdims=True)


def _fox_decode(pt, q, kn, vn, lfn, kc, vc, lfc, *, pg):
    db, _, aw = q.shape
    nh = aw // HEAD_DIM
    page = kc.shape[1]
    npg = pt.shape[1]
    vec = pl.BlockSpec((1, 1, aw), lambda b, j, pt: (b, 0, 0))
    pspec = lambda shape, p: pl.BlockSpec(shape, lambda b, j, pt: (pt[b, j * pg + p], 0, 0))
    in_specs = [vec, vec, vec, pl.BlockSpec((1, nh, 1), lambda b, j, pt: (b, 0, 0))]
    in_specs += [pspec((1, page, aw), p) for p in range(pg)] * 2
    in_specs += [pspec((1, nh, page), p) for p in range(pg)]
    return pl.pallas_call(
        functools.partial(_fox_decode_kernel, pg=pg, nh=nh),
        grid_spec=pltpu.PrefetchScalarGridSpec(
            num_scalar_prefetch=1, grid=(db, npg // pg), in_specs=in_specs, out_specs=vec,
            scratch_shapes=[pltpu.VMEM((ROWS, 1), F32), pltpu.VMEM((ROWS, 1), F32), pltpu.VMEM((ROWS, aw), F32),
                            pltpu.VMEM((ROWS, 1), F32)]),
        out_shape=jax.ShapeDtypeStruct((db, 1, aw), F32),
        compiler_params=_params(("arbitrary", "arbitrary")),
        name="fox_decode",
    )(pt, q, kn, vn, lfn, *([kc] * pg), *([vc] * pg), *([lfc] * pg))


def _moba_gate_kernel(pt_ref, q_ref, *refs, pg, nh, n_blocks):
    k_refs = refs[:pg]
    sel_ref, km_ref = refs[pg:]
    j = pl.program_id(1)
    page = k_refs[0].shape[1]
    per_block = MOBA_BLOCK // page

    @pl.when(j == 0)
    def _():
        km_ref[...] = jnp.zeros(km_ref.shape, F32)

    for g in range(pg // per_block):
        tot = jnp.sum(k_refs[g * per_block][0], axis=0, keepdims=True)
        for u in range(1, per_block):
            tot = tot + jnp.sum(k_refs[g * per_block + u][0], axis=0, keepdims=True)
        km_ref[pl.ds(j * (pg // per_block) + g, 1), :] = tot * (1.0 / MOBA_BLOCK)

    @pl.when(j == pl.num_programs(1) - 1)
    def _():
        qrows, _ = _head_rows(q_ref[0], nh)
        gate = _nt(qrows, km_ref[...], precision=HIGHEST)
        lane = lax.broadcasted_iota(jnp.int32, gate.shape, 1)
        g = jnp.where(lane < n_blocks, gate, NEG_INF)
        lane_f = lane.astype(F32)
        out = jnp.zeros(gate.shape, F32)
        for t in range(MOBA_TOPK):
            m = jnp.max(g, axis=1, keepdims=True)
            idx = jnp.min(jnp.where(g == m, lane_f, 1e9), axis=1, keepdims=True)
            out = jnp.where(lane == t, idx, out)
            g = jnp.where(lane_f == idx, -3e38, g)
        sel_ref[0] = out[:nh].astype(jnp.int32)


def _moba_gate(pt, q, kc, *, pg):
    db, _, aw = q.shape
    nh = aw // HEAD_DIM
    page = kc.shape[1]
    npg = pt.shape[1]
    n_blocks = npg * page // MOBA_BLOCK
    in_specs = [pl.BlockSpec((1, 1, aw), lambda b, j, pt: (b, 0, 0))]
    in_specs += [pl.BlockSpec((1, page, aw), lambda b, j, pt, p=p: (pt[b, j * pg + p], 0, 0)) for p in range(pg)]
    return pl.pallas_call(
        functools.partial(_moba_gate_kernel, pg=pg, nh=nh, n_blocks=n_blocks),
        grid_spec=pltpu.PrefetchScalarGridSpec(
            num_scalar_prefetch=1, grid=(db, npg // pg), in_specs=in_specs,
            out_specs=pl.BlockSpec((1, nh, LANES), lambda b, j, pt: (b, 0, 0)),
            scratch_shapes=[pltpu.VMEM((LANES, aw), F32)]),
        out_shape=jax.ShapeDtypeStruct((db, nh, LANES), jnp.int32),
        compiler_params=_params(("arbitrary", "arbitrary")),
        name="moba_gate",
    )(pt, q, *([kc] * pg))


def _moba_decode_kernel(pt_ref, sel_ref, q_ref, kn_ref, vn_ref, *refs, n_pages):
    k_refs, v_refs = refs[:2 * n_pages], refs[2 * n_pages:4 * n_pages]
    o_ref = refs[4 * n_pages]
    q, kn, vn = q_ref[0], kn_ref[0], vn_ref[0]
    lane = lax.broadcasted_iota(jnp.int32, q.shape, 1)
    outs = []
    for e in range(2):
        own = (lane < HEAD_DIM) if e == 0 else (lane >= HEAD_DIM)
        qe = jnp.where(own, q, 0.0)
        q16 = jnp.broadcast_to(qe, (ROWS, LANES)).astype(BF16)
        s_new = jnp.sum(qe * kn, axis=1, keepdims=True)
        ss = [_nt(q16, k_refs[e * n_pages + r][0].astype(BF16)) for r in range(n_pages)]
        m = s_new
        for s in ss:
            m = jnp.maximum(m, jnp.max(s, axis=1, keepdims=True))
        p_new = jnp.exp(s_new - m)
        l = p_new
        acc = p_new * vn
        for r, s in enumerate(ss):
            p = jnp.exp(s - m)
            l = l + jnp.sum(p, axis=1, keepdims=True)
            acc = acc + _nn(p.astype(BF16), v_refs[e * n_pages + r][0].astype(BF16))
        outs.append((acc / l)[0:1])
    o_ref[0] = jnp.where(lane < HEAD_DIM, outs[0], outs[1])


def _moba_decode(pt, sel, q, kn, vn, kc, vc):
    db, _, aw = q.shape
    hp = aw // LANES
    page = kc.shape[1]
    per_block = MOBA_BLOCK // page
    n_pages = MOBA_TOPK * per_block
    vec = pl.BlockSpec((1, 1, LANES), lambda b, h, pt, sel: (b, 0, h))

    def pspec(e, r):
        t, u = divmod(r, per_block)
        return pl.BlockSpec(
            (1, page, LANES),
            lambda b, h, pt, sel: (pt[b, sel[b, (2 * h + e) * MOBA_TOPK + t] * per_block + u], 0, h))

    pages = [pspec(e, r) for e in range(2) for r in range(n_pages)]
    return pl.pallas_call(
        functools.partial(_moba_decode_kernel, n_pages=n_pages),
        grid_spec=pltpu.PrefetchScalarGridSpec(
            num_scalar_prefetch=2, grid=(db, hp), in_specs=[vec, vec, vec] + pages + pages, out_specs=vec),
        out_shape=jax.ShapeDtypeStruct((db, 1, aw), F32),
        compiler_params=_params(("arbitrary", "arbitrary")),
        name="moba_decode",
    )(pt, sel, q, kn, vn, *([kc] * (2 * n_pages)), *([vc] * (2 * n_pages)))


def _rope_tables(pos):
    inv = ROPE_THETA ** (-jnp.arange(HALF, dtype=F32) / HALF)
    ang = pos.astype(F32)[:, None] * inv[None, :]
    cos, sin, zero = jnp.cos(ang), jnp.sin(ang), jnp.zeros_like(ang)
    reps = LANES // HEAD_DIM
    return (jnp.tile(cos, (1, 2 * reps)), jnp.tile(jnp.concatenate([-sin, zero], axis=1), (1, reps)),
            jnp.tile(jnp.concatenate([zero, sin], axis=1), (1, reps)))


def _pick(n, candidates):
    for c in candidates:
        if n % c == 0:
            return c
    raise ValueError(f"no tile for {n}")


def kernel(x_prompt, x_sample, cache_fox_k, cache_fox_v, cache_fox_logf, cache_moba_k, cache_moba_v, state_ffn_conv, page_table, norm_mix_g, w_in, b_forget, w_fox_o, w_moba_o, w_out, norm_ffn_g, w_up, conv_w, conv_b, w_down, norm_final_g):
    batch, seq, d = x_prompt.shape
    db, dseq, _ = x_sample.shape
    depth, n_pool, page, nh, hd = cache_fox_k.shape
    aw = nh * hd
    dff = w_down.shape[1]
    n_pages = page_table.shape[1]
    past = n_pages * page
    assert depth == 1 and hd == HEAD_DIM and cache_moba_k.shape == cache_fox_k.shape and dseq == 1
    assert seq // MOBA_BLOCK <= HEAD_DIM and past // MOBA_BLOCK <= LANES
    assert seq % MOBA_BLOCK == 0 and past % MOBA_BLOCK == 0 and past // MOBA_BLOCK >= MOBA_TOPK
    assert MOBA_BLOCK % page == 0 and aw % LANES == 0 and nh <= 8

    n = batch * seq
    tm = _pick(seq, (512, 256, 128))
    tf = _pick(dff, (1408, 1024, 512, 256, 128))
    pg = _pick(n_pages, (8, 4, 2))
    tables_p = _rope_tables(jnp.arange(seq))
    tables_s = _rope_tables(jnp.full((db,), past))

    xp = x_prompt.reshape(n, d)
    xs = x_sample.reshape(db, d)
    fk = cache_fox_k.reshape(depth * n_pool, page, aw)
    fv = cache_fox_v.reshape(depth * n_pool, page, aw)
    fl = jnp.swapaxes(cache_fox_logf.reshape(depth * n_pool, page, nh), 1, 2)
    mk = cache_moba_k.reshape(depth * n_pool, page, aw)
    mv = cache_moba_v.reshape(depth * n_pool, page, aw)
    sizes = (aw, aw, aw, nh, aw, aw, aw, d, d)
    offs = [sum(sizes[:i]) for i in range(len(sizes) + 1)]

    p_new, s_new = [], []
    for l in range(depth):
        cols = [w_in[l][:, offs[i]:offs[i + 1]] for i in range(len(sizes))]
        qa, ka, va, fa, qb, kb, vb, ga, gb = cols
        w_cat = jnp.concatenate([qa * ATTN_SCALE, ka, va, qb * ATTN_SCALE, kb, vb, ga, gb,
                                 jnp.pad(fa, ((0, 0), (0, LANES - nh)))], axis=1).astype(BF16)
        bf = jnp.pad(b_forget[l].astype(F32), (0, LANES - nh)).reshape(1, LANES)
        g_mix = norm_mix_g[l].reshape(1, d)
        g_ffn = norm_ffn_g[l].reshape(1, d)
        wfo, wmo, wout = w_fox_o[l].astype(BF16), w_moba_o[l].astype(BF16), w_out[l].astype(BF16)
        wup, wd = w_up[l].astype(BF16), w_down[l].astype(BF16)
        cw, cb = conv_w[l], conv_b[l].reshape(1, 2 * dff)
        gf = norm_final_g.reshape(1, d)
        pt = page_table + l * n_pool

        qa_p, ka_p, va_p, qb_p, kb_p, vb_p, ga_p, gb_p, lf_p, lft_p = _inproj(
            xp, g_mix, w_cat, bf, *tables_p, tm=tm, aw=aw, nh=nh, emit_lft=True)
        oa_p = _fox_prompt(qa_p, ka_p, va_p, lft_p, batch=batch, seq=seq, t=MOBA_BLOCK)
        ob_p = _moba_prompt(qb_p, kb_p, vb_p, batch=batch, seq=seq)
        x1_p, h2_p = _merge(oa_p, ob_p, ga_p, gb_p, xp, wfo, wmo, wout, g_ffn, tm=tm)
        xp, ta, tg = _ffn_prompt(h2_p, x1_p, wup, cw, cb, wd, gf, batch=batch, seq=seq, tm=tm, tf=tf)
        tps = seq // tm
        tail_p = jnp.concatenate([ta[tps - 1::tps, 8 - (CONV_W - 1):], tg[tps - 1::tps, 8 - (CONV_W - 1):]], axis=-1)
        p_new.append((ka_p.reshape(batch, seq, nh, hd), va_p.reshape(batch, seq, nh, hd),
                      lf_p.reshape(batch, seq, nh), kb_p.reshape(batch, seq, nh, hd),
                      vb_p.reshape(batch, seq, nh, hd), tail_p))

        qa_s, ka_s, va_s, qb_s, kb_s, vb_s, ga_s, gb_s, lf_s = _inproj(
            xs, g_mix, w_cat, bf, *tables_s, tm=db, aw=aw, nh=nh, emit_lft=False)
        r3 = lambda t: t.reshape(db, 1, aw)
        oa_s = _fox_decode(pt, r3(qa_s), r3(ka_s), r3(va_s), lf_s.reshape(db, nh, 1), fk, fv, fl, pg=pg)
        sel = _moba_gate(pt, r3(qb_s), mk, pg=pg)[:, :, :MOBA_TOPK].reshape(db, nh * MOBA_TOPK)
        ob_s = _moba_decode(pt, sel, r3(qb_s), r3(kb_s), r3(vb_s), mk, mv)
        x1_s, h2_s = _merge(oa_s.reshape(db, aw), ob_s.reshape(db, aw), ga_s, gb_s, xs, wfo, wmo, wout, g_ffn, tm=db)
        prev = state_ffn_conv[l]
        xs, ua, ug = _ffn_sample(h2_s, x1_s, prev.reshape(db, (CONV_W - 1) * 2 * dff), wup, cw, cb, wd, gf, tf=tf)
        tail_s = jnp.concatenate([prev[:, 1:], jnp.concatenate([ua, ug], axis=-1)[:, None]], axis=1)
        s_new.append((ka_s.reshape(db, 1, nh, hd), va_s.reshape(db, 1, nh, hd), lf_s.reshape(db, 1, nh),
                      kb_s.reshape(db, 1, nh, hd), vb_s.reshape(db, 1, nh, hd), tail_s))

    p_out = [jnp.stack(z) for z in zip(*p_new)]
    s_out = [jnp.stack(z) for z in zip(*s_new)]
    return (xp.reshape(batch, seq, d), xs.reshape(db, 1, d), *p_out, *s_out)
```

```python
import functools

import jax
import jax.numpy as jnp
from jax import lax
from jax.experimental import pallas as pl
from jax.experimental.pallas import tpu as pltpu

HEAD_DIM = 64
HALF = HEAD_DIM // 2
LANES = 128
MOBA_BLOCK = 256
MOBA_TOPK = 3
CONV_W = 3
ROPE_THETA = 10000.0
RMS_EPS = 1e-6
NEG_INF = -1e30
MASK_BIAS = -2e30
ATTN_SCALE = HEAD_DIM ** -0.5
LOG2E = 1.4426950408889634
VMEM_LIMIT = 56 * 1024 * 1024

F32 = jnp.float32
BF16 = jnp.bfloat16
HIGHEST = lax.Precision.HIGHEST


def _nt(a, b, precision=None):
    return lax.dot_general(a, b, (((1,), (1,)), ((), ())), preferred_element_type=F32, precision=precision)


def _nn(a, b, precision=None):
    return jnp.dot(a, b, preferred_element_type=F32, precision=precision)


def _params(sem):
    return pltpu.CompilerParams(dimension_semantics=sem, vmem_limit_bytes=VMEM_LIMIT)


def _resident(shape):
    nd = len(shape)
    return pl.BlockSpec(shape, lambda *_: (0,) * nd, pipeline_mode=pl.Buffered(1))


def _inproj_kernel(x_ref, g_ref, w_ref, bf_ref, cos_ref, sina_ref, sinb_ref,
                   qa_ref, ka_ref, va_ref, qb_ref, kb_ref, vb_ref, ga_ref, gb_ref, lf_ref, *rest,
                   aw, d_model, emit_lft):
    x = x_ref[...]
    h = (x * lax.rsqrt(jnp.mean(x * x, axis=-1, keepdims=True) + RMS_EPS) * g_ref[...]).astype(BF16)

    def proj(c0, width):
        return _nn(h, w_ref[:, c0:c0 + width])

    qa_ref[...] = proj(0, aw)
    ka_ref[...] = proj(aw, aw)
    va_ref[...] = proj(2 * aw, aw)

    cos, sina, sinb = cos_ref[...], sina_ref[...], sinb_ref[...]

    def rope_store(t, o_ref):
        for s in range(aw // LANES):
            xs = t[:, s * LANES:(s + 1) * LANES]
            o_ref[:, s * LANES:(s + 1) * LANES] = (
                xs * cos + pltpu.roll(xs, LANES - HALF, 1) * sina + pltpu.roll(xs, HALF, 1) * sinb)

    rope_store(proj(3 * aw, aw), qb_ref)
    rope_store(proj(4 * aw, aw), kb_ref)
    vb_ref[...] = proj(5 * aw, aw)
    ga_ref[...] = jax.nn.sigmoid(proj(6 * aw, d_model))
    gb_ref[...] = jax.nn.sigmoid(proj(6 * aw + d_model, d_model))

    f = proj(6 * aw + 2 * d_model, LANES) + bf_ref[...]
    lsig = -(jnp.maximum(-f, 0.0) + jnp.log1p(jnp.exp(-jnp.abs(f))))
    nh = lf_ref.shape[-1]
    lf_ref[...] = lsig[:, :nh]
    if emit_lft:
        rest[0][...] = lsig.T[:nh, :]


def _inproj(x, g, w, bf, cos, sina, sinb, *, tm, aw, nh, emit_lft):
    n, d = x.shape
    nt = n // tm
    tb = cos.shape[0] // tm
    row = lambda i: (i, 0)
    tbl = lambda i: (i % tb, 0)
    out_shape = [jax.ShapeDtypeStruct((n, aw), F32)] * 6 + [jax.ShapeDtypeStruct((n, d), F32)] * 2 + [
        jax.ShapeDtypeStruct((n, nh), F32)]
    out_specs = [pl.BlockSpec((tm, aw), row)] * 6 + [pl.BlockSpec((tm, d), row)] * 2 + [pl.BlockSpec((tm, nh), row)]
    if emit_lft:
        out_shape.append(jax.ShapeDtypeStruct((nh, n), F32))
        out_specs.append(pl.BlockSpec((nh, tm), lambda i: (0, i)))
    return pl.pallas_call(
        functools.partial(_inproj_kernel, aw=aw, d_model=d, emit_lft=emit_lft),
        grid=(nt,),
        in_specs=[pl.BlockSpec((tm, d), row), _resident(g.shape), _resident(w.shape), _resident(bf.shape),
                  pl.BlockSpec((tm, LANES), tbl), pl.BlockSpec((tm, LANES), tbl), pl.BlockSpec((tm, LANES), tbl)],
        out_specs=out_specs,
        out_shape=out_shape,
        compiler_params=_params(("arbitrary",)),
        name="inproj",
    )(x, g, w, bf, cos, sina, sinb)


def _attn_init(m_ref, l_ref, acc_ref):
    m_ref[...] = jnp.full(m_ref.shape, NEG_INF, F32)
    l_ref[...] = jnp.zeros(l_ref.shape, F32)
    acc_ref[...] = jnp.zeros(acc_ref.shape, F32)


def _flash_causal(i, qts, k_refs, vt_ref, o_ref, *, t):
    def block(j, carry, masked=False):
        off = pl.multiple_of(j * t, t)
        out = []
        for e in range(2):
            m_old, l_old, acc = carry[e]
            s = _nn(k_refs[e][pl.ds(off, t), :], qts[e])
            if masked:
                kpos = lax.broadcasted_iota(jnp.int32, (t, t), 0)
                qpos = lax.broadcasted_iota(jnp.int32, (t, t), 1)
                s = jnp.where(kpos <= qpos, s, NEG_INF)
            m_new = jnp.maximum(m_old, jnp.max(s, axis=0, keepdims=True))
            alpha = jnp.exp2(m_old - m_new)
            p = jnp.exp2(s - m_new)
            l_new = alpha * l_old + jnp.sum(p, axis=0, keepdims=True)
            vt = vt_ref[e * HEAD_DIM:(e + 1) * HEAD_DIM, pl.ds(off, t)]
            out.append((m_new, l_new, alpha * acc + _nn(vt, p.astype(BF16))))
        return tuple(out)

    init = tuple((jnp.full((1, t), NEG_INF, F32), jnp.zeros((1, t), F32), jnp.zeros((HEAD_DIM, t), F32))
                 for _ in range(2))
    carry = lax.fori_loop(0, i // 2, lambda jj, c: block(2 * jj + 1, block(2 * jj, c)), init)
    carry = lax.cond(i % 2 == 1, lambda c: block(i - 1, c), lambda c: c, carry)
    carry = block(i, carry, masked=True)
    o_ref[...] = jnp.concatenate([acc / l for _, l, acc in carry], axis=0).T


def _split3(x):
    hi = x.astype(BF16).astype(F32)
    mid = (x - hi).astype(BF16).astype(F32)
    lo = (x - hi - mid).astype(BF16).astype(F32)
    return hi, mid, lo


def _fox_prompt_kernel(q_ref, k_ref, v_ref, lft_ref, o_ref, ka16, kb16, vt16, *, t, seq):
    i = pl.program_id(2)

    @pl.when(i == 0)
    def _():
        c = lft_ref[0]
        pos = lax.broadcasted_iota(jnp.int32, c.shape, 1)
        sh = 1
        while sh < seq:
            c = c + jnp.where(pos >= sh, pltpu.roll(c, sh, 1), 0.0)
            sh *= 2
        sub = lax.broadcasted_iota(jnp.int32, (8, seq), 0)
        rows = []
        for e in range(2):
            parts = _split3(-LOG2E * c[e:e + 1])
            rows.append(sum(jnp.where(sub == r, jnp.broadcast_to(part, (8, seq)), 0.0)
                            for r, part in enumerate(parts)))
        fill = jnp.zeros((HEAD_DIM - 8, seq), F32)
        decay = jnp.concatenate([rows[1], fill, rows[0], fill], axis=0).T
        k = k_ref[...]
        lane = lax.broadcasted_iota(jnp.int32, k.shape, 1)
        ka16[...] = jnp.where(lane < HEAD_DIM, k, decay).astype(BF16)
        kb16[...] = jnp.where(lane >= HEAD_DIM, k, decay).astype(BF16)
        vt16[...] = v_ref[...].T.astype(BF16)

    qt = (q_ref[...] * LOG2E).T
    ones = jnp.where(lax.broadcasted_iota(jnp.int32, (8, t), 0) < 3, 1.0, 0.0)
    fill = jnp.zeros((HEAD_DIM - 8, t), F32)
    qts = (jnp.concatenate([qt[:HEAD_DIM], ones, fill], axis=0).astype(BF16),
           jnp.concatenate([ones, fill, qt[HEAD_DIM:]], axis=0).astype(BF16))
    _flash_causal(i, qts, (ka16, kb16), vt16, o_ref, t=t)


def _fox_prompt(q, k, v, lft, *, batch, seq, t):
    n, aw = q.shape
    hp = aw // LANES
    nq = seq // t
    lft3 = lft.reshape(hp, 2, n)
    qmap = lambda b, h, i: (b * nq + i, h)
    kmap = lambda b, h, i: (b, h)
    return pl.pallas_call(
        functools.partial(_fox_prompt_kernel, t=t, seq=seq),
        grid=(batch, hp, nq),
        in_specs=[pl.BlockSpec((t, LANES), qmap), pl.BlockSpec((seq, LANES), kmap), pl.BlockSpec((seq, LANES), kmap),
                  pl.BlockSpec((1, 2, seq), lambda b, h, i: (h, 0, b))],
        out_specs=pl.BlockSpec((t, LANES), qmap),
        out_shape=jax.ShapeDtypeStruct((n, aw), F32),
        scratch_shapes=[pltpu.VMEM((seq, LANES), BF16), pltpu.VMEM((seq, LANES), BF16), pltpu.VMEM((LANES, seq), BF16)],
        compiler_params=_params(("arbitrary", "arbitrary", "arbitrary")),
        name="fox_prompt",
    )(q, k, v, lft3)


def _block_bias(gate, limit):
    n = lax.broadcasted_iota(jnp.int32, gate.shape, 0)
    nf = n.astype(F32)
    g = jnp.where(n < limit, gate, NEG_INF)
    sel = jnp.zeros(gate.shape, F32)
    for _ in range(MOBA_TOPK):
        m = jnp.max(g, axis=0, keepdims=True)
        idx = jnp.min(jnp.where(g == m, nf, 1e9), axis=0, keepdims=True)
        hit = nf == idx
        sel = jnp.where(hit & (idx < limit), 1.0, sel)
        g = jnp.where(hit, -3e38, g)
    return jnp.where((n < limit) & (sel < 0.5), MASK_BIAS, 0.0)


def _moba_prompt_kernel(q_ref, k_ref, v_ref, o_ref, ka16, kb16, vt16, km_ref, *, seq, nbp, t):
    bs = MOBA_BLOCK
    nb = seq // bs
    i = pl.program_id(2)

    @pl.when(i == 0)
    def _():
        k = k_ref[...]
        lane = lax.broadcasted_iota(jnp.int32, k.shape, 1)
        blk = lax.broadcasted_iota(jnp.int32, k.shape, 0) // bs
        ka16[...] = jnp.where(lane < HEAD_DIM, k, jnp.where(lane - HEAD_DIM == blk, 1.0, 0.0)).astype(BF16)
        kb16[...] = jnp.where(lane >= HEAD_DIM, k, jnp.where(lane == blk, 1.0, 0.0)).astype(BF16)
        vt16[...] = v_ref[...].T.astype(BF16)
        km_ref[...] = jnp.zeros(km_ref.shape, F32)
        for n in range(nb):
            km_ref[n:n + 1, :] = jnp.sum(k[n * bs:(n + 1) * bs, :], axis=0, keepdims=True) * (1.0 / bs)

    qt = q_ref[...].T
    km = km_ref[...]
    lane = lax.broadcasted_iota(jnp.int32, km.shape, 1)
    fill = jnp.zeros((HEAD_DIM - nbp, t), F32)
    own = (i * t + lax.broadcasted_iota(jnp.int32, (1, t), 1)) // bs
    bias_a = _block_bias(_nn(jnp.where(lane < HEAD_DIM, km, 0.0), qt, precision=HIGHEST), own)
    bias_b = _block_bias(_nn(jnp.where(lane >= HEAD_DIM, km, 0.0), qt, precision=HIGHEST), own)
    qs = qt * LOG2E
    qts = (jnp.concatenate([qs[:HEAD_DIM], bias_a, fill], axis=0).astype(BF16),
           jnp.concatenate([bias_b, fill, qs[HEAD_DIM:]], axis=0).astype(BF16))
    _flash_causal(i, qts, (ka16, kb16), vt16, o_ref, t=t)


def _moba_prompt(q, k, v, *, batch, seq, t):
    n, aw = q.shape
    hp = aw // LANES
    nq = seq // t
    nbp = -(-(seq // MOBA_BLOCK) // 8) * 8
    qmap = lambda b, h, i: (b * nq + i, h)
    kmap = lambda b, h, i: (b, h)
    return pl.pallas_call(
        functools.partial(_moba_prompt_kernel, seq=seq, nbp=nbp, t=t),
        grid=(batch, hp, nq),
        in_specs=[pl.BlockSpec((t, LANES), qmap), pl.BlockSpec((seq, LANES), kmap), pl.BlockSpec((seq, LANES), kmap)],
        out_specs=pl.BlockSpec((t, LANES), qmap),
        out_shape=jax.ShapeDtypeStruct((n, aw), F32),
        scratch_shapes=[pltpu.VMEM((seq, LANES), BF16), pltpu.VMEM((seq, LANES), BF16), pltpu.VMEM((LANES, seq), BF16),
                        pltpu.VMEM((nbp, LANES), F32)],
        compiler_params=_params(("arbitrary", "arbitrary", "arbitrary")),
        name="moba_prompt",
    )(q, k, v)


def _merge_kernel(oa_ref, ob_ref, ga_ref, gb_ref, x_ref, wfo_ref, wmo_ref, wout_ref, g_ref, x1_ref, h_ref):
    pa = _nn(oa_ref[...].astype(BF16), wfo_ref[...])
    pb = _nn(ob_ref[...].astype(BF16), wmo_ref[...])
    mix = ga_ref[...] * pa + gb_ref[...] * pb
    x1 = x_ref[...] + _nn(mix.astype(BF16), wout_ref[...])
    x1_ref[...] = x1
    h_ref[...] = (x1 * lax.rsqrt(jnp.mean(x1 * x1, axis=-1, keepdims=True) + RMS_EPS) * g_ref[...]).astype(BF16)


def _merge(oa, ob, ga, gb, x, wfo, wmo, wout, g, *, tm):
    n, d = x.shape
    aw = oa.shape[1]
    row = lambda i: (i, 0)
    return pl.pallas_call(
        _merge_kernel,
        grid=(n // tm,),
        in_specs=[pl.BlockSpec((tm, aw), row), pl.BlockSpec((tm, aw), row), pl.BlockSpec((tm, d), row),
                  pl.BlockSpec((tm, d), row), pl.BlockSpec((tm, d), row),
                  _resident(wfo.shape), _resident(wmo.shape), _resident(wout.shape), _resident(g.shape)],
        out_specs=[pl.BlockSpec((tm, d), row), pl.BlockSpec((tm, d), row)],
        out_shape=[jax.ShapeDtypeStruct((n, d), F32), jax.ShapeDtypeStruct((n, d), BF16)],
        compiler_params=_params(("arbitrary",)),
        name="merge",
    )(oa, ob, ga, gb, x, wfo, wmo, wout, g)


def _gated_act(ca, cg):
    gelu = 0.5 * ca * (1.0 + lax.erf(ca * (2.0 ** -0.5)))
    return (gelu * cg).astype(BF16)


def _ffn_prompt_kernel(h_ref, x1_ref, wa_ref, wg_ref, cwa_ref, cwg_ref, cba_ref, cbg_ref, wd_ref, gf_ref,
                       y_ref, ta_ref, tg_ref, acc_ref, prev_ref, *, tiles_per_seq, n_chunks):
    i = pl.program_id(0)
    c = pl.program_id(1)
    h = h_ref[...]
    tm = h.shape[0]
    seq_start = i % tiles_per_seq == 0
    row = lax.broadcasted_iota(jnp.int32, (8, wa_ref.shape[1]), 0)

    def conv(w_ref, cw_ref, cb_ref, slot, tail_ref):
        u = _nn(h, w_ref[...])
        last = u[tm - 8:, :]
        tail_ref[0] = last
        @pl.when(seq_start)
        def _():
            prev_ref[slot] = jnp.zeros(last.shape, F32)

        prev = prev_ref[slot]
        prev_ref[slot] = last
        s1 = pltpu.roll(u, 1, 0)
        s2 = pltpu.roll(u, 2, 0)
        s1 = jnp.concatenate([jnp.where(row < 1, pltpu.roll(prev, 1, 0), s1[:8]), s1[8:]], axis=0)
        s2 = jnp.concatenate([jnp.where(row < 2, pltpu.roll(prev, 2, 0), s2[:8]), s2[8:]], axis=0)
        cw = cw_ref[...]
        return cb_ref[...] + cw[0:1] * s2 + cw[1:2] * s1 + cw[2:3] * u

    ca = conv(wa_ref, cwa_ref, cba_ref, c, ta_ref)
    cg = conv(wg_ref, cwg_ref, cbg_ref, n_chunks + c, tg_ref)
    part = _nn(_gated_act(ca, cg), wd_ref[...])

    @pl.when(c == 0)
    def _():
        acc_ref[...] = part

    @pl.when(c > 0)
    def _():
        acc_ref[...] += part

    @pl.when(c == n_chunks - 1)
    def _():
        xo = x1_ref[...] + acc_ref[...]
        y_ref[...] = xo * lax.rsqrt(jnp.mean(xo * xo, axis=-1, keepdims=True) + RMS_EPS) * gf_ref[...]


def _ffn_prompt(h, x1, wup, cw, cb, wd, gf, *, batch, seq, tm, tf):
    n, d = x1.shape
    dff = wd.shape[0]
    nc = dff // tf
    tps = seq // tm
    row = lambda i, c: (i, 0)
    ca = lambda i, c: (0, c)
    cg = lambda i, c: (0, nc + c)
    tail = lambda i, c: (i, 0, c)
    return pl.pallas_call(
        functools.partial(_ffn_prompt_kernel, tiles_per_seq=tps, n_chunks=nc),
        grid=(n // tm, nc),
        in_specs=[pl.BlockSpec((tm, d), row), pl.BlockSpec((tm, d), row),
                  pl.BlockSpec((d, tf), ca), pl.BlockSpec((d, tf), cg),
                  pl.BlockSpec((CONV_W, tf), ca), pl.BlockSpec((CONV_W, tf), cg),
                  pl.BlockSpec((1, tf), ca), pl.BlockSpec((1, tf), cg),
                  pl.BlockSpec((tf, d), lambda i, c: (c, 0)), pl.BlockSpec((1, d), lambda i, c: (0, 0))],
        out_specs=[pl.BlockSpec((tm, d), row), pl.BlockSpec((1, 8, tf), tail), pl.BlockSpec((1, 8, tf), tail)],
        out_shape=[jax.ShapeDtypeStruct((n, d), F32), jax.ShapeDtypeStruct((n // tm, 8, dff), F32),
                   jax.ShapeDtypeStruct((n // tm, 8, dff), F32)],
        scratch_shapes=[pltpu.VMEM((tm, d), F32), pltpu.VMEM((2 * nc, 8, tf), F32)],
        compiler_params=_params(("arbitrary", "arbitrary")),
        name="ffn_prompt",
    )(h, x1, wup, wup, cw, cw, cb, cb, wd, gf)


def _ffn_sample_kernel(h_ref, x1_ref, p0a_ref, p0g_ref, p1a_ref, p1g_ref, wa_ref, wg_ref, cwa_ref, cwg_ref,
                       cba_ref, cbg_ref, wd_ref, gf_ref, y_ref, ua_ref, ug_ref, acc_ref, *, n_chunks):
    c = pl.program_id(0)
    h = h_ref[...]

    def conv(w_ref, cw_ref, cb_ref, p0_ref, p1_ref, u_ref):
        u = _nn(h, w_ref[...])
        u_ref[...] = u
        cw = cw_ref[...]
        return cb_ref[...] + cw[0:1] * p0_ref[...] + cw[1:2] * p1_ref[...] + cw[2:3] * u

    ca = conv(wa_ref, cwa_ref, cba_ref, p0a_ref, p1a_ref, ua_ref)
    cg = conv(wg_ref, cwg_ref, cbg_ref, p0g_ref, p1g_ref, ug_ref)
    part = _nn(_gated_act(ca, cg), wd_ref[...])

    @pl.when(c == 0)
    def _():
        acc_ref[...] = part

    @pl.when(c > 0)
    def _():
        acc_ref[...] += part

    @pl.when(c == n_chunks - 1)
    def _():
        xo = x1_ref[...] + acc_ref[...]
        y_ref[...] = xo * lax.rsqrt(jnp.mean(xo * xo, axis=-1, keepdims=True) + RMS_EPS) * gf_ref[...]


def _ffn_sample(h, x1, prev, wup, cw, cb, wd, gf, *, tf):
    n, d = x1.shape
    dff = wd.shape[0]
    nc = dff // tf
    full = lambda c: (0, 0)
    col = lambda k: (lambda c: (0, k * nc + c))
    return pl.pallas_call(
        functools.partial(_ffn_sample_kernel, n_chunks=nc),
        grid=(nc,),
        in_specs=[pl.BlockSpec((n, d), full), pl.BlockSpec((n, d), full),
                  pl.BlockSpec((n, tf), col(0)), pl.BlockSpec((n, tf), col(1)),
                  pl.BlockSpec((n, tf), col(2)), pl.BlockSpec((n, tf), col(3)),
                  pl.BlockSpec((d, tf), col(0)), pl.BlockSpec((d, tf), col(1)),
                  pl.BlockSpec((CONV_W, tf), col(0)), pl.BlockSpec((CONV_W, tf), col(1)),
                  pl.BlockSpec((1, tf), col(0)), pl.BlockSpec((1, tf), col(1)),
                  pl.BlockSpec((tf, d), lambda c: (c, 0)), pl.BlockSpec((1, d), full)],
        out_specs=[pl.BlockSpec((n, d), full), pl.BlockSpec((n, tf), col(0)), pl.BlockSpec((n, tf), col(0))],
        out_shape=[jax.ShapeDtypeStruct((n, d), F32), jax.ShapeDtypeStruct((n, dff), F32),
                   jax.ShapeDtypeStruct((n, dff), F32)],
        scratch_shapes=[pltpu.VMEM((n, d), F32)],
        compiler_params=_params(("arbitrary",)),
        name="ffn_sample",
    )(h, x1, prev, prev, prev, prev, wup, wup, cw, cw, cb, cb, wd, gf)


ROWS = 16


def _pad_rows(x):
    return jnp.concatenate([x, jnp.zeros((ROWS - x.shape[0],) + x.shape[1:], x.dtype)], axis=0)


def _head_slab(ref, h, page, nh):
    return ref[0, pl.ds(h, page, stride=nh), :]


def _fox_decode_kernel(pt_ref, q_ref, kn_ref, vn_ref, lfn_ref, *refs, pg, nh, page):
    k_refs, v_refs, lf_refs = refs[:pg], refs[pg:2 * pg], refs[2 * pg:3 * pg]
    o_ref, m_ref, l_ref, acc_ref, carry_ref = refs[3 * pg:]
    j = pl.program_id(1)

    @pl.when(j == 0)
    def _():
        _attn_init(m_ref, l_ref, acc_ref)
        carry_ref[...] = jnp.zeros(carry_ref.shape, F32)

    q8 = q_ref[0]
    q16 = _pad_rows(q8).astype(BF16)
    tri = (lax.broadcasted_iota(jnp.int32, (page, page), 0) <= lax.broadcasted_iota(jnp.int32, (page, page), 1))
    tri = tri.astype(F32)
    sub_s = lax.broadcasted_iota(jnp.int32, (ROWS, page), 0)
    sub_o = lax.broadcasted_iota(jnp.int32, (ROWS, HEAD_DIM), 0)

    carry = carry_ref[...]
    scores = []
    for p in range(pg):
        c = carry + _nn(_pad_rows(lf_refs[p][0]), tri, precision=HIGHEST)
        carry = c[:, page - 1:page]
        s = -c
        for h in range(nh):
            s = s + jnp.where(sub_s == h, _nt(q16, _head_slab(k_refs[p], h, page, nh).astype(BF16)), 0.0)
        scores.append(s)
    carry_ref[...] = carry
    s = jnp.concatenate(scores, axis=1)
    m_old = m_ref[...]
    m_new = jnp.maximum(m_old, jnp.max(s, axis=1, keepdims=True))
    alpha = jnp.exp(m_old - m_new)
    pr = jnp.exp(s - m_new)
    p16 = pr.astype(BF16)
    pv = jnp.zeros((ROWS, HEAD_DIM), F32)
    for p in range(pg):
        pp = p16[:, p * page:(p + 1) * page]
        for h in range(nh):
            pv = pv + jnp.where(sub_o == h, _nn(pp, _head_slab(v_refs[p], h, page, nh).astype(BF16)), 0.0)
    l_ref[...] = alpha * l_ref[...] + jnp.sum(pr, axis=1, keepdims=True)
    acc_ref[...] = alpha * acc_ref[...] + pv
    m_ref[...] = m_new

    @pl.when(j == pl.num_programs(1) - 1)
    def _():
        s = _pad_rows(jnp.sum(q8 * kn_ref[0], axis=1, keepdims=True) - lfn_ref[0]) - carry_ref[...]
        m_old = m_ref[...]
        m_new = jnp.maximum(m_old, s)
        alpha = jnp.exp(m_old - m_new)
        p = jnp.exp(s - m_new)
        l = alpha * l_ref[...] + p
        acc = alpha * acc_ref[...] + p * _pad_rows(vn_ref[0])
        o_ref[0] = (acc / l)[:nh]


def _fox_decode(pt, q, kn, vn, lfn, kc, vc, lfc, *, pg, page):
    db, nh, hd = q.shape
    npg = pt.shape[1]
    vec = pl.BlockSpec((1, nh, hd), lambda b, j, pt: (b, 0, 0))
    pspec = lambda shape, p: pl.BlockSpec(shape, lambda b, j, pt: (pt[b, j * pg + p], 0, 0))
    in_specs = [vec, vec, vec, pl.BlockSpec((1, nh, 1), lambda b, j, pt: (b, 0, 0))]
    in_specs += [pspec((1, page * nh, hd), p) for p in range(pg)] * 2
    in_specs += [pspec((1, nh, page), p) for p in range(pg)]
    return pl.pallas_call(
        functools.partial(_fox_decode_kernel, pg=pg, nh=nh, page=page),
        grid_spec=pltpu.PrefetchScalarGridSpec(
            num_scalar_prefetch=1, grid=(db, npg // pg), in_specs=in_specs, out_specs=vec,
            scratch_shapes=[pltpu.VMEM((ROWS, 1), F32), pltpu.VMEM((ROWS, 1), F32), pltpu.VMEM((ROWS, hd), F32),
                            pltpu.VMEM((ROWS, 1), F32)]),
        out_shape=jax.ShapeDtypeStruct((db, nh, hd), F32),
        compiler_params=_params(("arbitrary", "arbitrary")),
        name="fox_decode",
    )(pt, q, kn, vn, lfn, *([kc] * pg), *([vc] * pg), *([lfc] * pg))


def _moba_gate_kernel(pt_ref, q_ref, *refs, pg, nh, page):
    k_refs = refs[:pg]
    sel_ref, km_ref = refs[pg:]
    j = pl.program_id(1)
    per_block = MOBA_BLOCK // page

    def page_sum(ref):
        def body(r, tot):
            return tot + ref[0, pl.ds(pl.multiple_of(r * nh, nh), nh), :]
        return lax.fori_loop(0, page, body, jnp.zeros((nh, HEAD_DIM), F32), unroll=8)

    for g in range(pg // per_block):
        tot = page_sum(k_refs[g * per_block])
        for u in range(1, per_block):
            tot = tot + page_sum(k_refs[g * per_block + u])
        km_ref[j * (pg // per_block) + g] = tot * (1.0 / MOBA_BLOCK)

    @pl.when(j == pl.num_programs(1) - 1)
    def _():
        g = jnp.sum(km_ref[...] * q_ref[0][None], axis=2, keepdims=True)
        n_f = lax.broadcasted_iota(jnp.int32, g.shape, 0).astype(F32)
        lane = lax.broadcasted_iota(jnp.int32, (nh, LANES), 1)
        out = jnp.zeros((nh, LANES), F32)
        for t in range(MOBA_TOPK):
            m = jnp.max(g, axis=0, keepdims=True)
            idx = jnp.min(jnp.where(g == m, n_f, 1e9), axis=0, keepdims=True)
            out = jnp.where(lane == t, idx[0], out)
            g = jnp.where(n_f == idx, -3e38, g)
        sel_ref[0] = out.astype(jnp.int32)


def _moba_gate(pt, q, kc, *, pg, page):
    db, nh, hd = q.shape
    npg = pt.shape[1]
    n_blocks = npg * page // MOBA_BLOCK
    in_specs = [pl.BlockSpec((1, nh, hd), lambda b, j, pt: (b, 0, 0))]
    in_specs += [pl.BlockSpec((1, page * nh, hd), lambda b, j, pt, p=p: (pt[b, j * pg + p], 0, 0)) for p in range(pg)]
    return pl.pallas_call(
        functools.partial(_moba_gate_kernel, pg=pg, nh=nh, page=page),
        grid_spec=pltpu.PrefetchScalarGridSpec(
            num_scalar_prefetch=1, grid=(db, npg // pg), in_specs=in_specs,
            out_specs=pl.BlockSpec((1, nh, LANES), lambda b, j, pt: (b, 0, 0)),
            scratch_shapes=[pltpu.VMEM((n_blocks, nh, hd), F32)]),
        out_shape=jax.ShapeDtypeStruct((db, nh, LANES), jnp.int32),
        compiler_params=_params(("arbitrary", "arbitrary")),
        name="moba_gate",
    )(pt, q, *([kc] * pg))


def _moba_decode_kernel(pt_ref, sel_ref, q_ref, kn_ref, vn_ref, *refs, n_pages, nh, page):
    k_refs, v_refs = refs[:n_pages], refs[n_pages:2 * n_pages]
    o_ref = refs[2 * n_pages]
    h = pl.program_id(1)
    q, kn, vn = q_ref[0, pl.ds(h, 1), :], kn_ref[0, pl.ds(h, 1), :], vn_ref[0, pl.ds(h, 1), :]
    q16 = jnp.broadcast_to(q, (ROWS, HEAD_DIM)).astype(BF16)
    s_new = jnp.sum(q * kn, axis=1, keepdims=True)
    ss = [_nt(q16, _head_slab(k_refs[r], h, page, nh).astype(BF16)) for r in range(n_pages)]
    m = s_new
    for s in ss:
        m = jnp.maximum(m, jnp.max(s, axis=1, keepdims=True))
    p_new = jnp.exp(s_new - m)
    l = p_new
    acc = p_new * vn
    for r, s in enumerate(ss):
        p = jnp.exp(s - m)
        l = l + jnp.sum(p, axis=1, keepdims=True)
        acc = acc + _nn(p.astype(BF16), _head_slab(v_refs[r], h, page, nh).astype(BF16))
    o_ref[0, pl.ds(h, 1), :] = (acc / l)[0:1]


def _moba_decode(pt, sel, q, kn, vn, kc, vc, *, page):
    db, nh, hd = q.shape
    per_block = MOBA_BLOCK // page
    n_pages = MOBA_TOPK * per_block
    vec = pl.BlockSpec((1, nh, hd), lambda b, h, pt, sel: (b, 0, 0))

    def pspec(r):
        t, u = divmod(r, per_block)
        return pl.BlockSpec((1, page * nh, hd),
                            lambda b, h, pt, sel: (pt[b, sel[b, h * MOBA_TOPK + t] * per_block + u], 0, 0))

    pages = [pspec(r) for r in range(n_pages)]
    return pl.pallas_call(
        functools.partial(_moba_decode_kernel, n_pages=n_pages, nh=nh, page=page),
        grid_spec=pltpu.PrefetchScalarGridSpec(
            num_scalar_prefetch=2, grid=(db, nh), in_specs=[vec, vec, vec] + pages + pages, out_specs=vec),
        out_shape=jax.ShapeDtypeStruct((db, nh, hd), F32),
        compiler_params=_params(("arbitrary", "arbitrary")),
        name="moba_decode",
    )(pt, sel, q, kn, vn, *([kc] * n_pages), *([vc] * n_pages))


def _rope_tables(pos):
    inv = ROPE_THETA ** (-jnp.arange(HALF, dtype=F32) / HALF)
    ang = pos.astype(F32)[:, None] * inv[None, :]
    cos, sin, zero = jnp.cos(ang), jnp.sin(ang), jnp.zeros_like(ang)
    reps = LANES // HEAD_DIM
    return (jnp.tile(cos, (1, 2 * reps)), jnp.tile(jnp.concatenate([-sin, zero], axis=1), (1, reps)),
            jnp.tile(jnp.concatenate([zero, sin], axis=1), (1, reps)))


def _pick(n, candidates):
    for c in candidates:
        if n % c == 0:
            return c
    raise ValueError(f"no tile for {n}")


def kernel(x_prompt, x_sample, cache_fox_k, cache_fox_v, cache_fox_logf, cache_moba_k, cache_moba_v, state_ffn_conv, page_table, norm_mix_g, w_in, b_forget, w_fox_o, w_moba_o, w_out, norm_ffn_g, w_up, conv_w, conv_b, w_down, norm_final_g):
    batch, seq, d = x_prompt.shape
    db, dseq, _ = x_sample.shape
    depth, n_pool, page, nh, hd = cache_fox_k.shape
    aw = nh * hd
    dff = w_down.shape[1]
    n_pages = page_table.shape[1]
    past = n_pages * page
    assert depth == 1 and hd == HEAD_DIM and cache_moba_k.shape == cache_fox_k.shape and dseq == 1
    assert seq // MOBA_BLOCK <= HEAD_DIM and past // MOBA_BLOCK <= LANES
    assert seq % MOBA_BLOCK == 0 and past % MOBA_BLOCK == 0 and past // MOBA_BLOCK >= MOBA_TOPK
    assert MOBA_BLOCK % page == 0 and aw % LANES == 0 and nh <= 8

    n = batch * seq
    tm = _pick(seq, (512, 256, 128))
    ta = _pick(seq, (512, MOBA_BLOCK))
    tf = _pick(dff, (1408, 1024, 512, 256, 128))
    pg = _pick(n_pages, (8, 4, 2))
    tables_p = _rope_tables(jnp.arange(seq))
    tables_s = _rope_tables(jnp.full((db,), past))

    xp = x_prompt.reshape(n, d)
    xs = x_sample.reshape(db, d)
    fk = cache_fox_k.reshape(depth * n_pool, page * nh, hd)
    fv = cache_fox_v.reshape(depth * n_pool, page * nh, hd)
    fl = jnp.swapaxes(cache_fox_logf.reshape(depth * n_pool, page, nh), 1, 2)
    mk = cache_moba_k.reshape(depth * n_pool, page * nh, hd)
    mv = cache_moba_v.reshape(depth * n_pool, page * nh, hd)
    sizes = (aw, aw, aw, nh, aw, aw, aw, d, d)
    offs = [sum(sizes[:i]) for i in range(len(sizes) + 1)]

    p_new, s_new = [], []
    for l in range(depth):
        cols = [w_in[l][:, offs[i]:offs[i + 1]] for i in range(len(sizes))]
        qa, ka, va, fa, qb, kb, vb, ga, gb = cols
        w_cat = jnp.concatenate([qa * ATTN_SCALE, ka, va, qb * ATTN_SCALE, kb, vb, ga, gb,
                                 jnp.pad(fa, ((0, 0), (0, LANES - nh)))], axis=1).astype(BF16)
        bf = jnp.pad(b_forget[l].astype(F32), (0, LANES - nh)).reshape(1, LANES)
        g_mix = norm_mix_g[l].reshape(1, d)
        g_ffn = norm_ffn_g[l].reshape(1, d)
        wfo, wmo, wout = w_fox_o[l].astype(BF16), w_moba_o[l].astype(BF16), w_out[l].astype(BF16)
        wup, wd = w_up[l].astype(BF16), w_down[l].astype(BF16)
        cw, cb = conv_w[l], conv_b[l].reshape(1, 2 * dff)
        gf = norm_final_g.reshape(1, d)
        pt = page_table + l * n_pool

        qa_p, ka_p, va_p, qb_p, kb_p, vb_p, ga_p, gb_p, lf_p, lft_p = _inproj(
            xp, g_mix, w_cat, bf, *tables_p, tm=tm, aw=aw, nh=nh, emit_lft=True)
        oa_p = _fox_prompt(qa_p, ka_p, va_p, lft_p, batch=batch, seq=seq, t=ta)
        ob_p = _moba_prompt(qb_p, kb_p, vb_p, batch=batch, seq=seq, t=ta)
        x1_p, h2_p = _merge(oa_p, ob_p, ga_p, gb_p, xp, wfo, wmo, wout, g_ffn, tm=tm)
        xp, ta, tg = _ffn_prompt(h2_p, x1_p, wup, cw, cb, wd, gf, batch=batch, seq=seq, tm=tm, tf=tf)
        tps = seq // tm
        tail_p = jnp.concatenate([ta[tps - 1::tps, 8 - (CONV_W - 1):], tg[tps - 1::tps, 8 - (CONV_W - 1):]], axis=-1)
        p_new.append((ka_p.reshape(batch, seq, nh, hd), va_p.reshape(batch, seq, nh, hd),
                      lf_p.reshape(batch, seq, nh), kb_p.reshape(batch, seq, nh, hd),
                      vb_p.reshape(batch, seq, nh, hd), tail_p))

        qa_s, ka_s, va_s, qb_s, kb_s, vb_s, ga_s, gb_s, lf_s = _inproj(
            xs, g_mix, w_cat, bf, *tables_s, tm=db, aw=aw, nh=nh, emit_lft=False)
        r3 = lambda t: t.reshape(db, nh, hd)
        oa_s = _fox_decode(pt, r3(qa_s), r3(ka_s), r3(va_s), lf_s.reshape(db, nh, 1), fk, fv, fl, pg=pg, page=page)
        sel = _moba_gate(pt, r3(qb_s), mk, pg=pg, page=page)[:, :, :MOBA_TOPK].reshape(db, nh * MOBA_TOPK)
        ob_s = _moba_decode(pt, sel, r3(qb_s), r3(kb_s), r3(vb_s), mk, mv, page=page)
        x1_s, h2_s = _merge(oa_s.reshape(db, aw), ob_s.reshape(db, aw), ga_s, gb_s, xs, wfo, wmo, wout, g_ffn, tm=db)
        prev = state_ffn_conv[l]
        xs, ua, ug = _ffn_sample(h2_s, x1_s, prev.reshape(db, (CONV_W - 1) * 2 * dff), wup, cw, cb, wd, gf, tf=tf)
        tail_s = jnp.concatenate([prev[:, 1:], jnp.concatenate([ua, ug], axis=-1)[:, None]], axis=1)
        s_new.append((ka_s.reshape(db, 1, nh, hd), va_s.reshape(db, 1, nh, hd), lf_s.reshape(db, 1, nh),
                      kb_s.reshape(db, 1, nh, hd), vb_s.reshape(db, 1, nh, hd), tail_s))

    p_out = [jnp.stack(z) for z in zip(*p_new)]
    s_out = [jnp.stack(z) for z in zip(*s_new)]
    return (xp.reshape(batch, seq, d), xs.reshape(db, 1, d), *p_out, *s_out)
```

```python
import functools

import jax
import jax.numpy as jnp
from jax import lax
from jax.experimental import pallas as pl
from jax.experimental.pallas import tpu as pltpu

HEAD_DIM = 64
HALF = HEAD_DIM // 2
LANES = 128
MOBA_BLOCK = 256
MOBA_TOPK = 3
CONV_W = 3
ROPE_THETA = 10000.0
RMS_EPS = 1e-6
NEG_INF = -1e30
MASK_BIAS = -2e30
ATTN_SCALE = HEAD_DIM ** -0.5
LOG2E = 1.4426950408889634
VMEM_LIMIT = 56 * 1024 * 1024

F32 = jnp.float32
BF16 = jnp.bfloat16
HIGHEST = lax.Precision.HIGHEST


def _nt(a, b, precision=None):
    return lax.dot_general(a, b, (((1,), (1,)), ((), ())), preferred_element_type=F32, precision=precision)


def _nn(a, b, precision=None):
    return jnp.dot(a, b, preferred_element_type=F32, precision=precision)


def _params(sem):
    return pltpu.CompilerParams(dimension_semantics=sem, vmem_limit_bytes=VMEM_LIMIT)


def _resident(shape):
    nd = len(shape)
    return pl.BlockSpec(shape, lambda *_: (0,) * nd, pipeline_mode=pl.Buffered(1))


def _inproj_kernel(x_ref, g_ref, w_ref, bf_ref, cos_ref, sina_ref, sinb_ref,
                   qa_ref, ka_ref, va_ref, qb_ref, kb_ref, vb_ref, ga_ref, gb_ref, lf_ref,
                   *, aw, d_model, nh, position_minor):
    x = x_ref[...]
    h = (x * lax.rsqrt(jnp.mean(x * x, axis=-1, keepdims=True) + RMS_EPS) * g_ref[...]).astype(BF16)

    def proj(c0, width):
        return _nn(h, w_ref[:, c0:c0 + width])

    def put(o_ref, val):
        if position_minor:
            o_ref[0] = val.T
        else:
            o_ref[...] = val

    cos, sina, sinb = cos_ref[...], sina_ref[...], sinb_ref[...]

    def rope(t):
        out = []
        for s in range(aw // LANES):
            xs = t[:, s * LANES:(s + 1) * LANES]
            out.append(xs * cos + pltpu.roll(xs, LANES - HALF, 1) * sina + pltpu.roll(xs, HALF, 1) * sinb)
        return jnp.concatenate(out, axis=1)

    qa_ref[...] = proj(0, aw)
    put(ka_ref, proj(aw, aw))
    put(va_ref, proj(2 * aw, aw))
    qb_ref[...] = rope(proj(3 * aw, aw))
    put(kb_ref, rope(proj(4 * aw, aw)))
    put(vb_ref, proj(5 * aw, aw))
    ga_ref[...] = jax.nn.sigmoid(proj(6 * aw, d_model))
    gb_ref[...] = jax.nn.sigmoid(proj(6 * aw + d_model, d_model))

    f = proj(6 * aw + 2 * d_model, LANES) + bf_ref[...]
    lsig = -(jnp.maximum(-f, 0.0) + jnp.log1p(jnp.exp(-jnp.abs(f))))
    if position_minor:
        lf_ref[0] = lsig.T[:nh, :]
    else:
        lf_ref[...] = lsig[:, :nh]


def _inproj(x, g, w, bf, cos, sina, sinb, *, tm, aw, nh, seq=None):
    n, d = x.shape
    nt = n // tm
    tb = cos.shape[0] // tm
    row = lambda i: (i, 0)
    tbl = lambda i: (i % tb, 0)
    nat = (jax.ShapeDtypeStruct((n, aw), F32), pl.BlockSpec((tm, aw), row))
    gate = (jax.ShapeDtypeStruct((n, d), F32), pl.BlockSpec((tm, d), row))
    if seq is None:
        kv = nat
        lf = (jax.ShapeDtypeStruct((n, nh), F32), pl.BlockSpec((tm, nh), row))
    else:
        tps = seq // tm
        pm = lambda i: (i // tps, 0, i % tps)
        kv = (jax.ShapeDtypeStruct((n // seq, aw, seq), F32), pl.BlockSpec((1, aw, tm), pm))
        lf = (jax.ShapeDtypeStruct((n // seq, nh, seq), F32), pl.BlockSpec((1, nh, tm), pm))
    outs = [nat, kv, kv, nat, kv, kv, gate, gate, lf]
    return pl.pallas_call(
        functools.partial(_inproj_kernel, aw=aw, d_model=d, nh=nh, position_minor=seq is not None),
        grid=(nt,),
        in_specs=[pl.BlockSpec((tm, d), row), _resident(g.shape), _resident(w.shape), _resident(bf.shape),
                  pl.BlockSpec((tm, LANES), tbl), pl.BlockSpec((tm, LANES), tbl), pl.BlockSpec((tm, LANES), tbl)],
        out_specs=[o[1] for o in outs],
        out_shape=[o[0] for o in outs],
        compiler_params=_params(("arbitrary",)),
        name="inproj",
    )(x, g, w, bf, cos, sina, sinb)


def _attn_init(m_ref, l_ref, acc_ref):
    m_ref[...] = jnp.full(m_ref.shape, NEG_INF, F32)
    l_ref[...] = jnp.zeros(l_ref.shape, F32)
    acc_ref[...] = jnp.zeros(acc_ref.shape, F32)


def _flash_causal(i, qts, k_refs, vt_ref, o_ref, *, t):
    def block(j, carry, masked=False):
        off = pl.multiple_of(j * t, t)
        out = []
        for e in range(2):
            m_old, l_old, acc = carry[e]
            s = _nn(k_refs[e][pl.ds(off, t), :], qts[e])
            if masked:
                kpos = lax.broadcasted_iota(jnp.int32, (t, t), 0)
                qpos = lax.broadcasted_iota(jnp.int32, (t, t), 1)
                s = jnp.where(kpos <= qpos, s, NEG_INF)
            m_new = jnp.maximum(m_old, jnp.max(s, axis=0, keepdims=True))
            alpha = jnp.exp2(m_old - m_new)
            p = jnp.exp2(s - m_new)
            l_new = alpha * l_old + jnp.sum(p, axis=0, keepdims=True)
            vt = vt_ref[e * HEAD_DIM:(e + 1) * HEAD_DIM, pl.ds(off, t)]
            out.append((m_new, l_new, alpha * acc + _nn(vt, p.astype(BF16))))
        return tuple(out)

    init = tuple((jnp.full((1, t), NEG_INF, F32), jnp.zeros((1, t), F32), jnp.zeros((HEAD_DIM, t), F32))
                 for _ in range(2))
    carry = lax.fori_loop(0, i // 2, lambda jj, c: block(2 * jj + 1, block(2 * jj, c)), init)
    carry = lax.cond(i % 2 == 1, lambda c: block(i - 1, c), lambda c: c, carry)
    carry = block(i, carry, masked=True)
    o_ref[...] = jnp.concatenate([acc / l for _, l, acc in carry], axis=0).T


def _split3(x):
    hi = x.astype(BF16).astype(F32)
    mid = (x - hi).astype(BF16).astype(F32)
    lo = (x - hi - mid).astype(BF16).astype(F32)
    return hi, mid, lo


def _fox_prompt_kernel(q_ref, k_ref, v_ref, lft_ref, o_ref, ka16, kb16, vt16, *, t, seq):
    i = pl.program_id(2)

    @pl.when(i == 0)
    def _():
        c = lft_ref[0, 0]
        pos = lax.broadcasted_iota(jnp.int32, c.shape, 1)
        sh = 1
        while sh < seq:
            c = c + jnp.where(pos >= sh, pltpu.roll(c, sh, 1), 0.0)
            sh *= 2
        sub = lax.broadcasted_iota(jnp.int32, (8, seq), 0)
        rows = []
        for e in range(2):
            parts = _split3(-LOG2E * c[e:e + 1])
            rows.append(sum(jnp.where(sub == r, jnp.broadcast_to(part, (8, seq)), 0.0)
                            for r, part in enumerate(parts)))
        fill = jnp.zeros((HEAD_DIM - 8, seq), F32)
        decay = jnp.concatenate([rows[1], fill, rows[0], fill], axis=0).T
        k = k_ref[0].T
        lane = lax.broadcasted_iota(jnp.int32, k.shape, 1)
        ka16[...] = jnp.where(lane < HEAD_DIM, k, decay).astype(BF16)
        kb16[...] = jnp.where(lane >= HEAD_DIM, k, decay).astype(BF16)
        vt16[...] = v_ref[0].astype(BF16)

    qt = (q_ref[...] * LOG2E).T
    ones = jnp.where(lax.broadcasted_iota(jnp.int32, (8, t), 0) < 3, 1.0, 0.0)
    fill = jnp.zeros((HEAD_DIM - 8, t), F32)
    qts = (jnp.concatenate([qt[:HEAD_DIM], ones, fill], axis=0).astype(BF16),
           jnp.concatenate([ones, fill, qt[HEAD_DIM:]], axis=0).astype(BF16))
    _flash_causal(i, qts, (ka16, kb16), vt16, o_ref, t=t)


def _fox_prompt(q, k, v, lft, *, batch, seq, t):
    n, aw = q.shape
    hp = aw // LANES
    nq = seq // t
    lft4 = lft.reshape(batch, hp, 2, seq)
    qmap = lambda b, h, i: (b * nq + i, h)
    kmap = lambda b, h, i: (b, h, 0)
    return pl.pallas_call(
        functools.partial(_fox_prompt_kernel, t=t, seq=seq),
        grid=(batch, hp, nq),
        in_specs=[pl.BlockSpec((t, LANES), qmap), pl.BlockSpec((1, LANES, seq), kmap), pl.BlockSpec((1, LANES, seq), kmap),
                  pl.BlockSpec((1, 1, 2, seq), lambda b, h, i: (b, h, 0, 0))],
        out_specs=pl.BlockSpec((t, LANES), qmap),
        out_shape=jax.ShapeDtypeStruct((n, aw), F32),
        scratch_shapes=[pltpu.VMEM((seq, LANES), BF16), pltpu.VMEM((seq, LANES), BF16), pltpu.VMEM((LANES, seq), BF16)],
        compiler_params=_params(("arbitrary", "arbitrary", "arbitrary")),
        name="fox_prompt",
    )(q, k, v, lft4)


def _block_bias(gate, limit):
    n = lax.broadcasted_iota(jnp.int32, gate.shape, 0)
    nf = n.astype(F32)
    g = jnp.where(n < limit, gate, NEG_INF)
    sel = jnp.zeros(gate.shape, F32)
    for _ in range(MOBA_TOPK):
        m = jnp.max(g, axis=0, keepdims=True)
        idx = jnp.min(jnp.where(g == m, nf, 1e9), axis=0, keepdims=True)
        hit = nf == idx
        sel = jnp.where(hit & (idx < limit), 1.0, sel)
        g = jnp.where(hit, -3e38, g)
    return jnp.where((n < limit) & (sel < 0.5), MASK_BIAS, 0.0)


def _moba_prompt_kernel(q_ref, k_ref, v_ref, o_ref, ka16, kb16, vt16, km_ref, *, seq, nbp, t):
    bs = MOBA_BLOCK
    nb = seq // bs
    i = pl.program_id(2)

    @pl.when(i == 0)
    def _():
        k = k_ref[0].T
        lane = lax.broadcasted_iota(jnp.int32, k.shape, 1)
        blk = lax.broadcasted_iota(jnp.int32, k.shape, 0) // bs
        ka16[...] = jnp.where(lane < HEAD_DIM, k, jnp.where(lane - HEAD_DIM == blk, 1.0, 0.0)).astype(BF16)
        kb16[...] = jnp.where(lane >= HEAD_DIM, k, jnp.where(lane == blk, 1.0, 0.0)).astype(BF16)
        vt16[...] = v_ref[0].astype(BF16)
        km_ref[...] = jnp.zeros(km_ref.shape, F32)
        for n in range(nb):
            km_ref[n:n + 1, :] = jnp.sum(k[n * bs:(n + 1) * bs, :], axis=0, keepdims=True) * (1.0 / bs)

    qt = q_ref[...].T
    km = km_ref[...]
    lane = lax.broadcasted_iota(jnp.int32, km.shape, 1)
    fill = jnp.zeros((HEAD_DIM - nbp, t), F32)
    own = (i * t + lax.broadcasted_iota(jnp.int32, (1, t), 1)) // bs
    bias_a = _block_bias(_nn(jnp.where(lane < HEAD_DIM, km, 0.0), qt, precision=HIGHEST), own)
    bias_b = _block_bias(_nn(jnp.where(lane >= HEAD_DIM, km, 0.0), qt, precision=HIGHEST), own)
    qs = qt * LOG2E
    qts = (jnp.concatenate([qs[:HEAD_DIM], bias_a, fill], axis=0).astype(BF16),
           jnp.concatenate([bias_b, fill, qs[HEAD_DIM:]], axis=0).astype(BF16))
    _flash_causal(i, qts, (ka16, kb16), vt16, o_ref, t=t)


def _moba_prompt(q, k, v, *, batch, seq, t):
    n, aw = q.shape
    hp = aw // LANES
    nq = seq // t
    nbp = -(-(seq // MOBA_BLOCK) // 8) * 8
    qmap = lambda b, h, i: (b * nq + i, h)
    kmap = lambda b, h, i: (b, h, 0)
    return pl.pallas_call(
        functools.partial(_moba_prompt_kernel, seq=seq, nbp=nbp, t=t),
        grid=(batch, hp, nq),
        in_specs=[pl.BlockSpec((t, LANES), qmap), pl.BlockSpec((1, LANES, seq), kmap), pl.BlockSpec((1, LANES, seq), kmap)],
        out_specs=pl.BlockSpec((t, LANES), qmap),
        out_shape=jax.ShapeDtypeStruct((n, aw), F32),
        scratch_shapes=[pltpu.VMEM((seq, LANES), BF16), pltpu.VMEM((seq, LANES), BF16), pltpu.VMEM((LANES, seq), BF16),
                        pltpu.VMEM((nbp, LANES), F32)],
        compiler_params=_params(("arbitrary", "arbitrary", "arbitrary")),
        name="moba_prompt",
    )(q, k, v)


def _merge_kernel(oa_ref, ob_ref, ga_ref, gb_ref, x_ref, wfo_ref, wmo_ref, wout_ref, g_ref, x1_ref, h_ref):
    pa = _nn(oa_ref[...].astype(BF16), wfo_ref[...])
    pb = _nn(ob_ref[...].astype(BF16), wmo_ref[...])
    mix = ga_ref[...] * pa + gb_ref[...] * pb
    x1 = x_ref[...] + _nn(mix.astype(BF16), wout_ref[...])
    x1_ref[...] = x1
    h_ref[...] = (x1 * lax.rsqrt(jnp.mean(x1 * x1, axis=-1, keepdims=True) + RMS_EPS) * g_ref[...]).astype(BF16)


def _merge(oa, ob, ga, gb, x, wfo, wmo, wout, g, *, tm):
    n, d = x.shape
    aw = oa.shape[1]
    row = lambda i: (i, 0)
    return pl.pallas_call(
        _merge_kernel,
        grid=(n // tm,),
        in_specs=[pl.BlockSpec((tm, aw), row), pl.BlockSpec((tm, aw), row), pl.BlockSpec((tm, d), row),
                  pl.BlockSpec((tm, d), row), pl.BlockSpec((tm, d), row),
                  _resident(wfo.shape), _resident(wmo.shape), _resident(wout.shape), _resident(g.shape)],
        out_specs=[pl.BlockSpec((tm, d), row), pl.BlockSpec((tm, d), row)],
        out_shape=[jax.ShapeDtypeStruct((n, d), F32), jax.ShapeDtypeStruct((n, d), BF16)],
        compiler_params=_params(("arbitrary",)),
        name="merge",
    )(oa, ob, ga, gb, x, wfo, wmo, wout, g)


def _gated_act(ca, cg):
    gelu = 0.5 * ca * (1.0 + lax.erf(ca * (2.0 ** -0.5)))
    return (gelu * cg).astype(BF16)


def _ffn_prompt_kernel(h_ref, x1_ref, wa_ref, wg_ref, cwa_ref, cwg_ref, cba_ref, cbg_ref, wd_ref, gf_ref,
                       y_ref, ta_ref, tg_ref, acc_ref, prev_ref, *, tiles_per_seq, n_chunks):
    i = pl.program_id(0)
    c = pl.program_id(1)
    h = h_ref[...]
    tm = h.shape[0]
    seq_start = i % tiles_per_seq == 0
    row = lax.broadcasted_iota(jnp.int32, (8, wa_ref.shape[1]), 0)

    def conv(w_ref, cw_ref, cb_ref, slot, tail_ref):
        u = _nn(h, w_ref[...])
        last = u[tm - 8:, :]
        tail_ref[0] = last
        @pl.when(seq_start)
        def _():
            prev_ref[slot] = jnp.zeros(last.shape, F32)

        prev = prev_ref[slot]
        prev_ref[slot] = last
        s1 = pltpu.roll(u, 1, 0)
        s2 = pltpu.roll(u, 2, 0)
        s1 = jnp.concatenate([jnp.where(row < 1, pltpu.roll(prev, 1, 0), s1[:8]), s1[8:]], axis=0)
        s2 = jnp.concatenate([jnp.where(row < 2, pltpu.roll(prev, 2, 0), s2[:8]), s2[8:]], axis=0)
        cw = cw_ref[...]
        return cb_ref[...] + cw[0:1] * s2 + cw[1:2] * s1 + cw[2:3] * u

    ca = conv(wa_ref, cwa_ref, cba_ref, c, ta_ref)
    cg = conv(wg_ref, cwg_ref, cbg_ref, n_chunks + c, tg_ref)
    part = _nn(_gated_act(ca, cg), wd_ref[...])

    @pl.when(c == 0)
    def _():
        acc_ref[...] = part

    @pl.when(c > 0)
    def _():
        acc_ref[...] += part

    @pl.when(c == n_chunks - 1)
    def _():
        xo = x1_ref[...] + acc_ref[...]
        y_ref[...] = xo * lax.rsqrt(jnp.mean(xo * xo, axis=-1, keepdims=True) + RMS_EPS) * gf_ref[...]


def _ffn_prompt(h, x1, wup, cw, cb, wd, gf, *, batch, seq, tm, tf):
    n, d = x1.shape
    dff = wd.shape[0]
    nc = dff // tf
    tps = seq // tm
    row = lambda i, c: (i, 0)
    ca = lambda i, c: (0, c)
    cg = lambda i, c: (0, nc + c)
    tail = lambda i, c: (i, 0, c)
    return pl.pallas_call(
        functools.partial(_ffn_prompt_kernel, tiles_per_seq=tps, n_chunks=nc),
        grid=(n // tm, nc),
        in_specs=[pl.BlockSpec((tm, d), row), pl.BlockSpec((tm, d), row),
                  pl.BlockSpec((d, tf), ca), pl.BlockSpec((d, tf), cg),
                  pl.BlockSpec((CONV_W, tf), ca), pl.BlockSpec((CONV_W, tf), cg),
                  pl.BlockSpec((1, tf), ca), pl.BlockSpec((1, tf), cg),
                  pl.BlockSpec((tf, d), lambda i, c: (c, 0)), pl.BlockSpec((1, d), lambda i, c: (0, 0))],
        out_specs=[pl.BlockSpec((tm, d), row), pl.BlockSpec((1, 8, tf), tail), pl.BlockSpec((1, 8, tf), tail)],
        out_shape=[jax.ShapeDtypeStruct((n, d), F32), jax.ShapeDtypeStruct((n // tm, 8, dff), F32),
                   jax.ShapeDtypeStruct((n // tm, 8, dff), F32)],
        scratch_shapes=[pltpu.VMEM((tm, d), F32), pltpu.VMEM((2 * nc, 8, tf), F32)],
        compiler_params=_params(("arbitrary", "arbitrary")),
        name="ffn_prompt",
    )(h, x1, wup, wup, cw, cw, cb, cb, wd, gf)


def _ffn_sample_kernel(h_ref, x1_ref, p0a_ref, p0g_ref, p1a_ref, p1g_ref, wa_ref, wg_ref, cwa_ref, cwg_ref,
                       cba_ref, cbg_ref, wd_ref, gf_ref, y_ref, ua_ref, ug_ref, acc_ref, *, n_chunks):
    c = pl.program_id(0)
    h = h_ref[...]

    def conv(w_ref, cw_ref, cb_ref, p0_ref, p1_ref, u_ref):
        u = _nn(h, w_ref[...])
        u_ref[...] = u
        cw = cw_ref[...]
        return cb_ref[...] + cw[0:1] * p0_ref[...] + cw[1:2] * p1_ref[...] + cw[2:3] * u

    ca = conv(wa_ref, cwa_ref, cba_ref, p0a_ref, p1a_ref, ua_ref)
    cg = conv(wg_ref, cwg_ref, cbg_ref, p0g_ref, p1g_ref, ug_ref)
    part = _nn(_gated_act(ca, cg), wd_ref[...])

    @pl.when(c == 0)
    def _():
        acc_ref[...] = part

    @pl.when(c > 0)
    def _():
        acc_ref[...] += part

    @pl.when(c == n_chunks - 1)
    def _():
        xo = x1_ref[...] + acc_ref[...]
        y_ref[...] = xo * lax.rsqrt(jnp.mean(xo * xo, axis=-1, keepdims=True) + RMS_EPS) * gf_ref[...]


def _ffn_sample(h, x1, prev, wup, cw, cb, wd, gf, *, tf):
    n, d = x1.shape
    dff = wd.shape[0]
    nc = dff // tf
    full = lambda c: (0, 0)
    col = lambda k: (lambda c: (0, k * nc + c))
    return pl.pallas_call(
        functools.partial(_ffn_sample_kernel, n_chunks=nc),
        grid=(nc,),
        in_specs=[pl.BlockSpec((n, d), full), pl.BlockSpec((n, d), full),
                  pl.BlockSpec((n, tf), col(0)), pl.BlockSpec((n, tf), col(1)),
                  pl.BlockSpec((n, tf), col(2)), pl.BlockSpec((n, tf), col(3)),
                  pl.BlockSpec((d, tf), col(0)), pl.BlockSpec((d, tf), col(1)),
                  pl.BlockSpec((CONV_W, tf), col(0)), pl.BlockSpec((CONV_W, tf), col(1)),
                  pl.BlockSpec((1, tf), col(0)), pl.BlockSpec((1, tf), col(1)),
                  pl.BlockSpec((tf, d), lambda c: (c, 0)), pl.BlockSpec((1, d), full)],
        out_specs=[pl.BlockSpec((n, d), full), pl.BlockSpec((n, tf), col(0)), pl.BlockSpec((n, tf), col(0))],
        out_shape=[jax.ShapeDtypeStruct((n, d), F32), jax.ShapeDtypeStruct((n, dff), F32),
                   jax.ShapeDtypeStruct((n, dff), F32)],
        scratch_shapes=[pltpu.VMEM((n, d), F32)],
        compiler_params=_params(("arbitrary",)),
        name="ffn_sample",
    )(h, x1, prev, prev, prev, prev, wup, wup, cw, cw, cb, cb, wd, gf)


ROWS = 16


def _pad_rows(x):
    return jnp.concatenate([x, jnp.zeros((ROWS - x.shape[0],) + x.shape[1:], x.dtype)], axis=0)


def _head_rows(q, nh):
    shape = (ROWS, q.shape[1])
    sub = lax.broadcasted_iota(jnp.int32, shape, 0)
    lane = lax.broadcasted_iota(jnp.int32, shape, 1)
    own = lane // HEAD_DIM == sub
    return jnp.where(own, jnp.broadcast_to(q, shape), 0.0), own


def _fox_decode_kernel(pt_ref, q_ref, kn_ref, vn_ref, lfn_ref, *refs, pg, nh, page):
    k_refs, v_refs, lf_refs = refs[:pg], refs[pg:2 * pg], refs[2 * pg:3 * pg]
    o_ref, m_ref, l_ref, acc_ref, carry_ref = refs[3 * pg:]
    j = pl.program_id(1)

    @pl.when(j == 0)
    def _():
        _attn_init(m_ref, l_ref, acc_ref)
        carry_ref[...] = jnp.zeros(carry_ref.shape, F32)

    qrows, own = _head_rows(q_ref[0], nh)
    q16 = qrows.astype(BF16)
    tri = (lax.broadcasted_iota(jnp.int32, (page, page), 0) <= lax.broadcasted_iota(jnp.int32, (page, page), 1))
    tri = tri.astype(F32)

    carry = carry_ref[...]
    scores = []
    for p in range(pg):
        c = carry + _nn(_pad_rows(lf_refs[p][0]), tri, precision=HIGHEST)
        carry = c[:, page - 1:page]
        scores.append(_nn(q16, k_refs[p][0].astype(BF16)) - c)
    carry_ref[...] = carry
    s = jnp.concatenate(scores, axis=1)
    m_old = m_ref[...]
    m_new = jnp.maximum(m_old, jnp.max(s, axis=1, keepdims=True))
    alpha = jnp.exp(m_old - m_new)
    pr = jnp.exp(s - m_new)
    p16 = pr.astype(BF16)
    pv = _nt(p16[:, :page], v_refs[0][0].astype(BF16))
    for p in range(1, pg):
        pv = pv + _nt(p16[:, p * page:(p + 1) * page], v_refs[p][0].astype(BF16))
    l_ref[...] = alpha * l_ref[...] + jnp.sum(pr, axis=1, keepdims=True)
    acc_ref[...] = alpha * acc_ref[...] + pv
    m_ref[...] = m_new

    @pl.when(j == pl.num_programs(1) - 1)
    def _():
        lfn = _pad_rows(lfn_ref[0])
        s = jnp.sum(qrows * kn_ref[0], axis=1, keepdims=True) - (carry_ref[...] + lfn)
        m_old = m_ref[...]
        m_new = jnp.maximum(m_old, s)
        alpha = jnp.exp(m_old - m_new)
        p = jnp.exp(s - m_new)
        l = alpha * l_ref[...] + p
        acc = alpha * acc_ref[...] + p * vn_ref[0]
        o_ref[0] = jnp.sum(jnp.where(own, acc / l, 0.0), axis=0, keepdims=True)


def _fox_decode(pt, q, kn, vn, lfn, kc, vc, lfc, *, pg):
    db, _, aw = q.shape
    nh = aw // HEAD_DIM
    page = kc.shape[2]
    npg = pt.shape[1]
    vec = pl.BlockSpec((1, 1, aw), lambda b, j, pt: (b, 0, 0))
    pspec = lambda shape, p: pl.BlockSpec(shape, lambda b, j, pt: (pt[b, j * pg + p], 0, 0))
    in_specs = [vec, vec, vec, pl.BlockSpec((1, nh, 1), lambda b, j, pt: (b, 0, 0))]
    in_specs += [pspec((1, aw, page), p) for p in range(pg)] * 2
    in_specs += [pspec((1, nh, page), p) for p in range(pg)]
    return pl.pallas_call(
        functools.partial(_fox_decode_kernel, pg=pg, nh=nh, page=page),
        grid_spec=pltpu.PrefetchScalarGridSpec(
            num_scalar_prefetch=1, grid=(db, npg // pg), in_specs=in_specs, out_specs=vec,
            scratch_shapes=[pltpu.VMEM((ROWS, 1), F32), pltpu.VMEM((ROWS, 1), F32), pltpu.VMEM((ROWS, aw), F32),
                            pltpu.VMEM((ROWS, 1), F32)]),
        out_shape=jax.ShapeDtypeStruct((db, 1, aw), F32),
        compiler_params=_params(("arbitrary", "arbitrary")),
        name="fox_decode",
    )(pt, q, kn, vn, lfn, *([kc] * pg), *([vc] * pg), *([lfc] * pg))


def _moba_gate_kernel(pt_ref, q_ref, *refs, pg, nh, page, n_blocks):
    k_refs = refs[:pg]
    sel_ref, km_ref = refs[pg:]
    j = pl.program_id(1)
    per_block = MOBA_BLOCK // page

    @pl.when(j == 0)
    def _():
        km_ref[...] = jnp.zeros(km_ref.shape, F32)

    km = km_ref[...]
    lane = lax.broadcasted_iota(jnp.int32, km.shape, 1)
    for g in range(pg // per_block):
        tot = k_refs[g * per_block][0]
        for u in range(1, per_block):
            tot = tot + k_refs[g * per_block + u][0]
        mean = jnp.sum(tot, axis=1, keepdims=True) * (1.0 / MOBA_BLOCK)
        km = jnp.where(lane == j * (pg // per_block) + g, mean, km)
    km_ref[...] = km

    @pl.when(j == pl.num_programs(1) - 1)
    def _():
        qrows, _ = _head_rows(q_ref[0], nh)
        gate = _nn(qrows, km_ref[...], precision=HIGHEST)
        lane = lax.broadcasted_iota(jnp.int32, gate.shape, 1)
        g = jnp.where(lane < n_blocks, gate, NEG_INF)
        lane_f = lane.astype(F32)
        out = jnp.zeros(gate.shape, F32)
        for t in range(MOBA_TOPK):
            m = jnp.max(g, axis=1, keepdims=True)
            idx = jnp.min(jnp.where(g == m, lane_f, 1e9), axis=1, keepdims=True)
            out = jnp.where(lane == t, idx, out)
            g = jnp.where(lane_f == idx, -3e38, g)
        sel_ref[0] = out[:nh].astype(jnp.int32)


def _moba_gate(pt, q, kc, *, pg):
    db, _, aw = q.shape
    nh = aw // HEAD_DIM
    page = kc.shape[2]
    npg = pt.shape[1]
    n_blocks = npg * page // MOBA_BLOCK
    in_specs = [pl.BlockSpec((1, 1, aw), lambda b, j, pt: (b, 0, 0))]
    in_specs += [pl.BlockSpec((1, aw, page), lambda b, j, pt, p=p: (pt[b, j * pg + p], 0, 0)) for p in range(pg)]
    return pl.pallas_call(
        functools.partial(_moba_gate_kernel, pg=pg, nh=nh, page=page, n_blocks=n_blocks),
        grid_spec=pltpu.PrefetchScalarGridSpec(
            num_scalar_prefetch=1, grid=(db, npg // pg), in_specs=in_specs,
            out_specs=pl.BlockSpec((1, nh, LANES), lambda b, j, pt: (b, 0, 0)),
            scratch_shapes=[pltpu.VMEM((aw, LANES), F32)]),
        out_shape=jax.ShapeDtypeStruct((db, nh, LANES), jnp.int32),
        compiler_params=_params(("arbitrary", "arbitrary")),
        name="moba_gate",
    )(pt, q, *([kc] * pg))


def _moba_decode_kernel(pt_ref, sel_ref, q_ref, kn_ref, vn_ref, *refs, n_pages):
    k_refs, v_refs = refs[:n_pages], refs[n_pages:2 * n_pages]
    o_ref = refs[2 * n_pages]
    h = pl.program_id(1)
    q, kn, vn = q_ref[0, pl.ds(h, 1), :], kn_ref[0, pl.ds(h, 1), :], vn_ref[0, pl.ds(h, 1), :]
    q16 = jnp.broadcast_to(q, (ROWS, HEAD_DIM)).astype(BF16)
    s_new = jnp.sum(q * kn, axis=1, keepdims=True)
    ss = [_nn(q16, k_refs[r][0].astype(BF16)) for r in range(n_pages)]
    m = s_new
    for s in ss:
        m = jnp.maximum(m, jnp.max(s, axis=1, keepdims=True))
    p_new = jnp.exp(s_new - m)
    l = p_new
    acc = p_new * vn
    for r, s in enumerate(ss):
        p = jnp.exp(s - m)
        l = l + jnp.sum(p, axis=1, keepdims=True)
        acc = acc + _nt(p.astype(BF16), v_refs[r][0].astype(BF16))
    o_ref[0, pl.ds(h, 1), :] = (acc / l)[0:1]


def _moba_decode(pt, sel, q, kn, vn, kc, vc):
    db, nh, hd = q.shape
    page = kc.shape[2]
    per_block = MOBA_BLOCK // page
    n_pages = MOBA_TOPK * per_block
    vec = pl.BlockSpec((1, nh, hd), lambda b, h, pt, sel: (b, 0, 0))

    def pspec(r):
        t, u = divmod(r, per_block)
        return pl.BlockSpec((1, hd, page),
                            lambda b, h, pt, sel: (pt[b, sel[b, h * MOBA_TOPK + t] * per_block + u], h, 0))

    pages = [pspec(r) for r in range(n_pages)]
    return pl.pallas_call(
        functools.partial(_moba_decode_kernel, n_pages=n_pages),
        grid_spec=pltpu.PrefetchScalarGridSpec(
            num_scalar_prefetch=2, grid=(db, nh), in_specs=[vec, vec, vec] + pages + pages, out_specs=vec),
        out_shape=jax.ShapeDtypeStruct((db, nh, hd), F32),
        compiler_params=_params(("arbitrary", "arbitrary")),
        name="moba_decode",
    )(pt, sel, q, kn, vn, *([kc] * n_pages), *([vc] * n_pages))


def _rope_tables(pos):
    inv = ROPE_THETA ** (-jnp.arange(HALF, dtype=F32) / HALF)
    ang = pos.astype(F32)[:, None] * inv[None, :]
    cos, sin, zero = jnp.cos(ang), jnp.sin(ang), jnp.zeros_like(ang)
    reps = LANES // HEAD_DIM
    return (jnp.tile(cos, (1, 2 * reps)), jnp.tile(jnp.concatenate([-sin, zero], axis=1), (1, reps)),
            jnp.tile(jnp.concatenate([zero, sin], axis=1), (1, reps)))


def _pick(n, candidates):
    for c in candidates:
        if n % c == 0:
            return c
    raise ValueError(f"no tile for {n}")


def kernel(x_prompt, x_sample, cache_fox_k, cache_fox_v, cache_fox_logf, cache_moba_k, cache_moba_v, state_ffn_conv, page_table, norm_mix_g, w_in, b_forget, w_fox_o, w_moba_o, w_out, norm_ffn_g, w_up, conv_w, conv_b, w_down, norm_final_g):
    batch, seq, d = x_prompt.shape
    db, dseq, _ = x_sample.shape
    depth, n_pool, page, nh, hd = cache_fox_k.shape
    aw = nh * hd
    dff = w_down.shape[1]
    n_pages = page_table.shape[1]
    past = n_pages * page
    assert depth == 1 and hd == HEAD_DIM and cache_moba_k.shape == cache_fox_k.shape and dseq == 1
    assert seq // MOBA_BLOCK <= HEAD_DIM and past // MOBA_BLOCK <= LANES
    assert seq % MOBA_BLOCK == 0 and past % MOBA_BLOCK == 0 and past // MOBA_BLOCK >= MOBA_TOPK
    assert MOBA_BLOCK % page == 0 and aw % LANES == 0 and nh <= 8

    n = batch * seq
    tm = _pick(seq, (512, 256, 128))
    t_attn = _pick(seq, (512, MOBA_BLOCK))
    tf = _pick(dff, (1408, 1024, 512, 256, 128))
    pg = _pick(n_pages, (8, 4, 2))
    tables_p = _rope_tables(jnp.arange(seq))
    tables_s = _rope_tables(jnp.full((db,), past))

    xp = x_prompt.reshape(n, d)
    xs = x_sample.reshape(db, d)
    feat_major = lambda c: jnp.transpose(c, (0, 1, 3, 4, 2)).reshape(depth * n_pool, aw, page)
    fk, fv, mk, mv = (feat_major(c) for c in (cache_fox_k, cache_fox_v, cache_moba_k, cache_moba_v))
    fl = jnp.transpose(cache_fox_logf, (0, 1, 3, 2)).reshape(depth * n_pool, nh, page)
    pos_major = lambda t: jnp.transpose(t.reshape(batch, nh, -1, seq), (0, 3, 1, 2))
    sizes = (aw, aw, aw, nh, aw, aw, aw, d, d)
    offs = [sum(sizes[:i]) for i in range(len(sizes) + 1)]

    p_new, s_new = [], []
    for l in range(depth):
        cols = [w_in[l][:, offs[i]:offs[i + 1]] for i in range(len(sizes))]
        qa, ka, va, fa, qb, kb, vb, ga, gb = cols
        w_cat = jnp.concatenate([qa * ATTN_SCALE, ka, va, qb * ATTN_SCALE, kb, vb, ga, gb,
                                 jnp.pad(fa, ((0, 0), (0, LANES - nh)))], axis=1).astype(BF16)
        bf = jnp.pad(b_forget[l].astype(F32), (0, LANES - nh)).reshape(1, LANES)
        g_mix = norm_mix_g[l].reshape(1, d)
        g_ffn = norm_ffn_g[l].reshape(1, d)
        wfo, wmo, wout = w_fox_o[l].astype(BF16), w_moba_o[l].astype(BF16), w_out[l].astype(BF16)
        wup, wd = w_up[l].astype(BF16), w_down[l].astype(BF16)
        cw, cb = conv_w[l], conv_b[l].reshape(1, 2 * dff)
        gf = norm_final_g.reshape(1, d)
        pt = page_table + l * n_pool

        qa_p, ka_p, va_p, qb_p, kb_p, vb_p, ga_p, gb_p, lf_p = _inproj(
            xp, g_mix, w_cat, bf, *tables_p, tm=tm, aw=aw, nh=nh, seq=seq)
        oa_p = _fox_prompt(qa_p, ka_p, va_p, lf_p, batch=batch, seq=seq, t=t_attn)
        ob_p = _moba_prompt(qb_p, kb_p, vb_p, batch=batch, seq=seq, t=t_attn)
        x1_p, h2_p = _merge(oa_p, ob_p, ga_p, gb_p, xp, wfo, wmo, wout, g_ffn, tm=tm)
        xp, ta, tg = _ffn_prompt(h2_p, x1_p, wup, cw, cb, wd, gf, batch=batch, seq=seq, tm=tm, tf=tf)
        tps = seq // tm
        tail_p = jnp.concatenate([ta[tps - 1::tps, 8 - (CONV_W - 1):], tg[tps - 1::tps, 8 - (CONV_W - 1):]], axis=-1)
        p_new.append((pos_major(ka_p), pos_major(va_p), jnp.transpose(lf_p, (0, 2, 1)), pos_major(kb_p),
                      pos_major(vb_p), tail_p))

        qa_s, ka_s, va_s, qb_s, kb_s, vb_s, ga_s, gb_s, lf_s = _inproj(
            xs, g_mix, w_cat, bf, *tables_s, tm=db, aw=aw, nh=nh)
        r1 = lambda t: t.reshape(db, 1, aw)
        r3 = lambda t: t.reshape(db, nh, hd)
        oa_s = _fox_decode(pt, r1(qa_s), r1(ka_s), r1(va_s), lf_s.reshape(db, nh, 1), fk, fv, fl, pg=pg)
        sel = _moba_gate(pt, r1(qb_s), mk, pg=pg)[:, :, :MOBA_TOPK].reshape(db, nh * MOBA_TOPK)
        ob_s = _moba_decode(pt, sel, r3(qb_s), r3(kb_s), r3(vb_s), mk, mv)
        x1_s, h2_s = _merge(oa_s.reshape(db, aw), ob_s.reshape(db, aw), ga_s, gb_s, xs, wfo, wmo, wout, g_ffn, tm=db)
        prev = state_ffn_conv[l]
        xs, ua, ug = _ffn_sample(h2_s, x1_s, prev.reshape(db, (CONV_W - 1) * 2 * dff), wup, cw, cb, wd, gf, tf=tf)
        tail_s = jnp.concatenate([prev[:, 1:], jnp.concatenate([ua, ug], axis=-1)[:, None]], axis=1)
        s_new.append((ka_s.reshape(db, 1, nh, hd), va_s.reshape(db, 1, nh, hd), lf_s.reshape(db, 1, nh),
                      kb_s.reshape(db, 1, nh, hd), vb_s.reshape(db, 1, nh, hd), tail_s))

    p_out = [jnp.stack(z) for z in zip(*p_new)]
    s_out = [jnp.stack(z) for z in zip(*s_new)]
    return (xp.reshape(batch, seq, d), xs.reshape(db, 1, d), *p_out, *s_out)
```

```python
import functools

import jax
import jax.numpy as jnp
from jax import lax
from jax.experimental import pallas as pl
from jax.experimental.pallas import tpu as pltpu

HEAD_DIM = 64
HALF = HEAD_DIM // 2
LANES = 128
MOBA_BLOCK = 256
MOBA_TOPK = 3
CONV_W = 3
ROPE_THETA = 10000.0
RMS_EPS = 1e-6
NEG_INF = -1e30
MASK_BIAS = -2e30
ATTN_SCALE = HEAD_DIM ** -0.5
LOG2E = 1.4426950408889634
VMEM_LIMIT = 56 * 1024 * 1024

F32 = jnp.float32
BF16 = jnp.bfloat16
HIGHEST = lax.Precision.HIGHEST


def _nt(a, b, precision=None):
    return lax.dot_general(a, b, (((1,), (1,)), ((), ())), preferred_element_type=F32, precision=precision)


def _nn(a, b, precision=None):
    return jnp.dot(a, b, preferred_element_type=F32, precision=precision)


def _params(sem):
    return pltpu.CompilerParams(dimension_semantics=sem, vmem_limit_bytes=VMEM_LIMIT)


def _resident(shape):
    nd = len(shape)
    return pl.BlockSpec(shape, lambda *_: (0,) * nd, pipeline_mode=pl.Buffered(1))


def _inproj_kernel(x_ref, g_ref, w_ref, bf_ref, cos_ref, sina_ref, sinb_ref,
                   qa_ref, ka_ref, va_ref, qb_ref, kb_ref, vb_ref, ga_ref, gb_ref, lf_ref,
                   *, aw, d_model, nh, position_minor):
    x = x_ref[...]
    h = (x * lax.rsqrt(jnp.mean(x * x, axis=-1, keepdims=True) + RMS_EPS) * g_ref[...]).astype(BF16)

    def proj(c0, width):
        return _nn(h, w_ref[:, c0:c0 + width])

    def put(o_ref, val):
        if position_minor:
            o_ref[0] = val.T
        else:
            o_ref[...] = val

    cos, sina, sinb = cos_ref[...], sina_ref[...], sinb_ref[...]

    def rope(t):
        out = []
        for s in range(aw // LANES):
            xs = t[:, s * LANES:(s + 1) * LANES]
            out.append(xs * cos + pltpu.roll(xs, LANES - HALF, 1) * sina + pltpu.roll(xs, HALF, 1) * sinb)
        return jnp.concatenate(out, axis=1)

    qa_ref[...] = proj(0, aw)
    put(ka_ref, proj(aw, aw))
    put(va_ref, proj(2 * aw, aw))
    qb_ref[...] = rope(proj(3 * aw, aw))
    put(kb_ref, rope(proj(4 * aw, aw)))
    put(vb_ref, proj(5 * aw, aw))
    ga_ref[...] = jax.nn.sigmoid(proj(6 * aw, d_model)).astype(ga_ref.dtype)
    gb_ref[...] = jax.nn.sigmoid(proj(6 * aw + d_model, d_model)).astype(gb_ref.dtype)

    f = proj(6 * aw + 2 * d_model, LANES) + bf_ref[...]
    lsig = -(jnp.maximum(-f, 0.0) + jnp.log1p(jnp.exp(-jnp.abs(f))))
    if position_minor:
        lf_ref[0] = lsig.T[:nh, :]
    else:
        lf_ref[...] = lsig[:, :nh]


def _inproj(x, g, w, bf, cos, sina, sinb, *, tm, aw, nh, seq=None):
    n, d = x.shape
    nt = n // tm
    tb = cos.shape[0] // tm
    row = lambda i: (i, 0)
    tbl = lambda i: (i % tb, 0)
    nat = (jax.ShapeDtypeStruct((n, aw), F32), pl.BlockSpec((tm, aw), row))
    gate = (jax.ShapeDtypeStruct((n, d), BF16), pl.BlockSpec((tm, d), row))
    if seq is None:
        kv = nat
        lf = (jax.ShapeDtypeStruct((n, nh), F32), pl.BlockSpec((tm, nh), row))
    else:
        tps = seq // tm
        pm = lambda i: (i // tps, 0, i % tps)
        kv = (jax.ShapeDtypeStruct((n // seq, aw, seq), F32), pl.BlockSpec((1, aw, tm), pm))
        lf = (jax.ShapeDtypeStruct((n // seq, nh, seq), F32), pl.BlockSpec((1, nh, tm), pm))
    outs = [nat, kv, kv, nat, kv, kv, gate, gate, lf]
    return pl.pallas_call(
        functools.partial(_inproj_kernel, aw=aw, d_model=d, nh=nh, position_minor=seq is not None),
        grid=(nt,),
        in_specs=[pl.BlockSpec((tm, d), row), _resident(g.shape), _resident(w.shape), _resident(bf.shape),
                  pl.BlockSpec((tm, LANES), tbl), pl.BlockSpec((tm, LANES), tbl), pl.BlockSpec((tm, LANES), tbl)],
        out_specs=[o[1] for o in outs],
        out_shape=[o[0] for o in outs],
        compiler_params=_params(("arbitrary",)),
        name="inproj",
    )(x, g, w, bf, cos, sina, sinb)


def _attn_init(m_ref, l_ref, acc_ref):
    m_ref[...] = jnp.full(m_ref.shape, NEG_INF, F32)
    l_ref[...] = jnp.zeros(l_ref.shape, F32)
    acc_ref[...] = jnp.zeros(acc_ref.shape, F32)


def _flash_causal(i, qts, k_refs, vt_ref, o_ref, *, t):
    def block(j, carry, masked=False):
        off = pl.multiple_of(j * t, t)
        out = []
        for e in range(2):
            m_old, l_old, acc = carry[e]
            s = _nn(k_refs[e][pl.ds(off, t), :], qts[e])
            if masked:
                kpos = lax.broadcasted_iota(jnp.int32, (t, t), 0)
                qpos = lax.broadcasted_iota(jnp.int32, (t, t), 1)
                s = jnp.where(kpos <= qpos, s, NEG_INF)
            m_new = jnp.maximum(m_old, jnp.max(s, axis=0, keepdims=True))
            alpha = jnp.exp2(m_old - m_new)
            p = jnp.exp2(s - m_new)
            l_new = alpha * l_old + jnp.sum(p, axis=0, keepdims=True)
            vt = vt_ref[e * HEAD_DIM:(e + 1) * HEAD_DIM, pl.ds(off, t)]
            out.append((m_new, l_new, alpha * acc + _nn(vt, p.astype(BF16))))
        return tuple(out)

    init = tuple((jnp.full((1, t), NEG_INF, F32), jnp.zeros((1, t), F32), jnp.zeros((HEAD_DIM, t), F32))
                 for _ in range(2))
    carry = lax.fori_loop(0, i // 2, lambda jj, c: block(2 * jj + 1, block(2 * jj, c)), init)
    carry = lax.cond(i % 2 == 1, lambda c: block(i - 1, c), lambda c: c, carry)
    carry = block(i, carry, masked=True)
    o_ref[...] = jnp.concatenate([acc / l for _, l, acc in carry], axis=0).T.astype(o_ref.dtype)


def _split3(x):
    hi = x.astype(BF16).astype(F32)
    mid = (x - hi).astype(BF16).astype(F32)
    lo = (x - hi - mid).astype(BF16).astype(F32)
    return hi, mid, lo


def _fox_prompt_kernel(q_ref, k_ref, v_ref, lft_ref, o_ref, ka16, kb16, vt16, *, t, seq):
    i = pl.program_id(2)

    @pl.when(i == 0)
    def _():
        c = lft_ref[0, 0]
        pos = lax.broadcasted_iota(jnp.int32, c.shape, 1)
        sh = 1
        while sh < seq:
            c = c + jnp.where(pos >= sh, pltpu.roll(c, sh, 1), 0.0)
            sh *= 2
        sub = lax.broadcasted_iota(jnp.int32, (8, seq), 0)
        rows = []
        for e in range(2):
            parts = _split3(-LOG2E * c[e:e + 1])
            rows.append(sum(jnp.where(sub == r, jnp.broadcast_to(part, (8, seq)), 0.0)
                            for r, part in enumerate(parts)))
        fill = jnp.zeros((HEAD_DIM - 8, seq), F32)
        decay = jnp.concatenate([rows[1], fill, rows[0], fill], axis=0).T
        k = k_ref[0].T
        lane = lax.broadcasted_iota(jnp.int32, k.shape, 1)
        ka16[...] = jnp.where(lane < HEAD_DIM, k, decay).astype(BF16)
        kb16[...] = jnp.where(lane >= HEAD_DIM, k, decay).astype(BF16)
        vt16[...] = v_ref[0].astype(BF16)

    qt = (q_ref[...] * LOG2E).T
    ones = jnp.where(lax.broadcasted_iota(jnp.int32, (8, t), 0) < 3, 1.0, 0.0)
    fill = jnp.zeros((HEAD_DIM - 8, t), F32)
    qts = (jnp.concatenate([qt[:HEAD_DIM], ones, fill], axis=0).astype(BF16),
           jnp.concatenate([ones, fill, qt[HEAD_DIM:]], axis=0).astype(BF16))
    _flash_causal(i, qts, (ka16, kb16), vt16, o_ref, t=t)


def _fox_prompt(q, k, v, lft, *, batch, seq, t):
    n, aw = q.shape
    hp = aw // LANES
    nq = seq // t
    lft4 = lft.reshape(batch, hp, 2, seq)
    qmap = lambda b, h, i: (b * nq + i, h)
    kmap = lambda b, h, i: (b, h, 0)
    return pl.pallas_call(
        functools.partial(_fox_prompt_kernel, t=t, seq=seq),
        grid=(batch, hp, nq),
        in_specs=[pl.BlockSpec((t, LANES), qmap), pl.BlockSpec((1, LANES, seq), kmap), pl.BlockSpec((1, LANES, seq), kmap),
                  pl.BlockSpec((1, 1, 2, seq), lambda b, h, i: (b, h, 0, 0))],
        out_specs=pl.BlockSpec((t, LANES), qmap),
        out_shape=jax.ShapeDtypeStruct((n, aw), BF16),
        scratch_shapes=[pltpu.VMEM((seq, LANES), BF16), pltpu.VMEM((seq, LANES), BF16), pltpu.VMEM((LANES, seq), BF16)],
        compiler_params=_params(("arbitrary", "arbitrary", "arbitrary")),
        name="fox_prompt",
    )(q, k, v, lft4)


def _block_bias(gate, limit):
    n = lax.broadcasted_iota(jnp.int32, gate.shape, 0)
    nf = n.astype(F32)
    g = jnp.where(n < limit, gate, NEG_INF)
    sel = jnp.zeros(gate.shape, F32)
    for _ in range(MOBA_TOPK):
        m = jnp.max(g, axis=0, keepdims=True)
        idx = jnp.min(jnp.where(g == m, nf, 1e9), axis=0, keepdims=True)
        hit = nf == idx
        sel = jnp.where(hit & (idx < limit), 1.0, sel)
        g = jnp.where(hit, -3e38, g)
    return jnp.where((n < limit) & (sel < 0.5), MASK_BIAS, 0.0)


def _moba_prompt_kernel(q_ref, k_ref, v_ref, o_ref, ka16, kb16, vt16, km_ref, *, seq, nbp, t):
    bs = MOBA_BLOCK
    nb = seq // bs
    i = pl.program_id(2)

    @pl.when(i == 0)
    def _():
        k = k_ref[0].T
        lane = lax.broadcasted_iota(jnp.int32, k.shape, 1)
        blk = lax.broadcasted_iota(jnp.int32, k.shape, 0) // bs
        ka16[...] = jnp.where(lane < HEAD_DIM, k, jnp.where(lane - HEAD_DIM == blk, 1.0, 0.0)).astype(BF16)
        kb16[...] = jnp.where(lane >= HEAD_DIM, k, jnp.where(lane == blk, 1.0, 0.0)).astype(BF16)
        vt16[...] = v_ref[0].astype(BF16)
        km_ref[...] = jnp.zeros(km_ref.shape, F32)
        for n in range(nb):
            km_ref[n:n + 1, :] = jnp.sum(k[n * bs:(n + 1) * bs, :], axis=0, keepdims=True) * (1.0 / bs)

    qt = q_ref[...].T
    km = km_ref[...]
    lane = lax.broadcasted_iota(jnp.int32, km.shape, 1)
    fill = jnp.zeros((HEAD_DIM - nbp, t), F32)
    own = (i * t + lax.broadcasted_iota(jnp.int32, (1, t), 1)) // bs
    bias_a = _block_bias(_nn(jnp.where(lane < HEAD_DIM, km, 0.0), qt, precision=HIGHEST), own)
    bias_b = _block_bias(_nn(jnp.where(lane >= HEAD_DIM, km, 0.0), qt, precision=HIGHEST), own)
    qs = qt * LOG2E
    qts = (jnp.concatenate([qs[:HEAD_DIM], bias_a, fill], axis=0).astype(BF16),
           jnp.concatenate([bias_b, fill, qs[HEAD_DIM:]], axis=0).astype(BF16))
    _flash_causal(i, qts, (ka16, kb16), vt16, o_ref, t=t)


def _moba_prompt(q, k, v, *, batch, seq, t):
    n, aw = q.shape
    hp = aw // LANES
    nq = seq // t
    nbp = -(-(seq // MOBA_BLOCK) // 8) * 8
    qmap = lambda b, h, i: (b * nq + i, h)
    kmap = lambda b, h, i: (b, h, 0)
    return pl.pallas_call(
        functools.partial(_moba_prompt_kernel, seq=seq, nbp=nbp, t=t),
        grid=(batch, hp, nq),
        in_specs=[pl.BlockSpec((t, LANES), qmap), pl.BlockSpec((1, LANES, seq), kmap), pl.BlockSpec((1, LANES, seq), kmap)],
        out_specs=pl.BlockSpec((t, LANES), qmap),
        out_shape=jax.ShapeDtypeStruct((n, aw), BF16),
        scratch_shapes=[pltpu.VMEM((seq, LANES), BF16), pltpu.VMEM((seq, LANES), BF16), pltpu.VMEM((LANES, seq), BF16),
                        pltpu.VMEM((nbp, LANES), F32)],
        compiler_params=_params(("arbitrary", "arbitrary", "arbitrary")),
        name="moba_prompt",
    )(q, k, v)


def _merge_kernel(oa_ref, ob_ref, ga_ref, gb_ref, x_ref, wfo_ref, wmo_ref, wout_ref, g_ref, x1_ref, h_ref):
    pa = _nn(oa_ref[...].astype(BF16), wfo_ref[...])
    pb = _nn(ob_ref[...].astype(BF16), wmo_ref[...])
    mix = ga_ref[...] * pa + gb_ref[...] * pb
    x1 = x_ref[...] + _nn(mix.astype(BF16), wout_ref[...])
    x1_ref[...] = x1
    h_ref[...] = (x1 * lax.rsqrt(jnp.mean(x1 * x1, axis=-1, keepdims=True) + RMS_EPS) * g_ref[...]).astype(BF16)


def _merge(oa, ob, ga, gb, x, wfo, wmo, wout, g, *, tm):
    n, d = x.shape
    aw = oa.shape[1]
    row = lambda i: (i, 0)
    return pl.pallas_call(
        _merge_kernel,
        grid=(n // tm,),
        in_specs=[pl.BlockSpec((tm, aw), row), pl.BlockSpec((tm, aw), row), pl.BlockSpec((tm, d), row),
                  pl.BlockSpec((tm, d), row), pl.BlockSpec((tm, d), row),
                  _resident(wfo.shape), _resident(wmo.shape), _resident(wout.shape), _resident(g.shape)],
        out_specs=[pl.BlockSpec((tm, d), row), pl.BlockSpec((tm, d), row)],
        out_shape=[jax.ShapeDtypeStruct((n, d), F32), jax.ShapeDtypeStruct((n, d), BF16)],
        compiler_params=_params(("arbitrary",)),
        name="merge",
    )(oa, ob, ga, gb, x, wfo, wmo, wout, g)


def _gated_act(ca, cg):
    gelu = 0.5 * ca * (1.0 + lax.erf(ca * (2.0 ** -0.5)))
    return (gelu * cg).astype(BF16)


def _ffn_prompt_kernel(h_ref, x1_ref, wup_ref, cw_ref, cb_ref, wd_ref, gf_ref, y_ref, tail_ref, prev_ref,
                       *, tiles_per_seq, dff, sw):
    i = pl.program_id(0)
    h = h_ref[...]
    tm = h.shape[0]
    row = lax.broadcasted_iota(jnp.int32, (8, sw), 0)

    @pl.when(i % tiles_per_seq == 0)
    def _():
        prev_ref[...] = jnp.zeros(prev_ref.shape, F32)

    def conv(c0):
        u = _nn(h, wup_ref[:, c0:c0 + sw])
        last = u[tm - 8:, :]
        tail_ref[0, :, c0:c0 + sw] = last
        prev = prev_ref[:, c0:c0 + sw]
        prev_ref[:, c0:c0 + sw] = last
        s1 = pltpu.roll(u, 1, 0)
        s2 = pltpu.roll(u, 2, 0)
        s1 = jnp.concatenate([jnp.where(row < 1, pltpu.roll(prev, 1, 0), s1[:8]), s1[8:]], axis=0)
        s2 = jnp.concatenate([jnp.where(row < 2, pltpu.roll(prev, 2, 0), s2[:8]), s2[8:]], axis=0)
        cw = cw_ref[:, c0:c0 + sw]
        return cb_ref[:, c0:c0 + sw] + cw[0:1] * s2 + cw[1:2] * s1 + cw[2:3] * u

    acts = [_gated_act(conv(c0), conv(dff + c0)) for c0 in range(0, dff, sw)]
    xo = x1_ref[...] + _nn(jnp.concatenate(acts, axis=1), wd_ref[...])
    y_ref[...] = xo * lax.rsqrt(jnp.mean(xo * xo, axis=-1, keepdims=True) + RMS_EPS) * gf_ref[...]


def _ffn_prompt(h, x1, wup, cw, cb, wd, gf, *, seq, tm, sw):
    n, d = x1.shape
    dff = wd.shape[0]
    row = lambda i: (i, 0)
    return pl.pallas_call(
        functools.partial(_ffn_prompt_kernel, tiles_per_seq=seq // tm, dff=dff, sw=sw),
        grid=(n // tm,),
        in_specs=[pl.BlockSpec((tm, d), row), pl.BlockSpec((tm, d), row), _resident(wup.shape), _resident(cw.shape),
                  _resident(cb.shape), _resident(wd.shape), _resident(gf.shape)],
        out_specs=[pl.BlockSpec((tm, d), row), pl.BlockSpec((1, 8, 2 * dff), lambda i: (i, 0, 0))],
        out_shape=[jax.ShapeDtypeStruct((n, d), F32), jax.ShapeDtypeStruct((n // tm, 8, 2 * dff), F32)],
        scratch_shapes=[pltpu.VMEM((8, 2 * dff), F32)],
        compiler_params=_params(("arbitrary",)),
        name="ffn_prompt",
    )(h, x1, wup, cw, cb, wd, gf)


def _ffn_sample_kernel(h_ref, x1_ref, p0a_ref, p0g_ref, p1a_ref, p1g_ref, wa_ref, wg_ref, cwa_ref, cwg_ref,
                       cba_ref, cbg_ref, wd_ref, gf_ref, y_ref, ua_ref, ug_ref, acc_ref, *, n_chunks):
    c = pl.program_id(0)
    h = h_ref[...]

    def conv(w_ref, cw_ref, cb_ref, p0_ref, p1_ref, u_ref):
        u = _nn(h, w_ref[...])
        u_ref[...] = u
        cw = cw_ref[...]
        return cb_ref[...] + cw[0:1] * p0_ref[...] + cw[1:2] * p1_ref[...] + cw[2:3] * u

    ca = conv(wa_ref, cwa_ref, cba_ref, p0a_ref, p1a_ref, ua_ref)
    cg = conv(wg_ref, cwg_ref, cbg_ref, p0g_ref, p1g_ref, ug_ref)
    part = _nn(_gated_act(ca, cg), wd_ref[...])

    @pl.when(c == 0)
    def _():
        acc_ref[...] = part

    @pl.when(c > 0)
    def _():
        acc_ref[...] += part

    @pl.when(c == n_chunks - 1)
    def _():
        xo = x1_ref[...] + acc_ref[...]
        y_ref[...] = xo * lax.rsqrt(jnp.mean(xo * xo, axis=-1, keepdims=True) + RMS_EPS) * gf_ref[...]


def _ffn_sample(h, x1, prev, wup, cw, cb, wd, gf, *, tf):
    n, d = x1.shape
    dff = wd.shape[0]
    nc = dff // tf
    full = lambda c: (0, 0)
    col = lambda k: (lambda c: (0, k * nc + c))
    return pl.pallas_call(
        functools.partial(_ffn_sample_kernel, n_chunks=nc),
        grid=(nc,),
        in_specs=[pl.BlockSpec((n, d), full), pl.BlockSpec((n, d), full),
                  pl.BlockSpec((n, tf), col(0)), pl.BlockSpec((n, tf), col(1)),
                  pl.BlockSpec((n, tf), col(2)), pl.BlockSpec((n, tf), col(3)),
                  pl.BlockSpec((d, tf), col(0)), pl.BlockSpec((d, tf), col(1)),
                  pl.BlockSpec((CONV_W, tf), col(0)), pl.BlockSpec((CONV_W, tf), col(1)),
                  pl.BlockSpec((1, tf), col(0)), pl.BlockSpec((1, tf), col(1)),
                  pl.BlockSpec((tf, d), lambda c: (c, 0)), pl.BlockSpec((1, d), full)],
        out_specs=[pl.BlockSpec((n, d), full), pl.BlockSpec((n, tf), col(0)), pl.BlockSpec((n, tf), col(0))],
        out_shape=[jax.ShapeDtypeStruct((n, d), F32), jax.ShapeDtypeStruct((n, dff), F32),
                   jax.ShapeDtypeStruct((n, dff), F32)],
        scratch_shapes=[pltpu.VMEM((n, d), F32)],
        compiler_params=_params(("arbitrary",)),
        name="ffn_sample",
    )(h, x1, prev, prev, prev, prev, wup, wup, cw, cw, cb, cb, wd, gf)


ROWS = 16


def _pad_rows(x):
    return jnp.concatenate([x, jnp.zeros((ROWS - x.shape[0],) + x.shape[1:], x.dtype)], axis=0)


def _head_rows(q, nh):
    shape = (ROWS, q.shape[1])
    sub = lax.broadcasted_iota(jnp.int32, shape, 0)
    lane = lax.broadcasted_iota(jnp.int32, shape, 1)
    own = lane // HEAD_DIM == sub
    return jnp.where(own, jnp.broadcast_to(q, shape), 0.0), own


def _decode_scan_kernel(pt_ref, q_ref, kn_ref, vn_ref, lfn_ref, qb_ref, *refs, pg, nh, page, n_blocks):
    k_refs, v_refs, lf_refs, mk_refs = (refs[r * pg:(r + 1) * pg] for r in range(4))
    o_ref, sel_ref, m_ref, l_ref, acc_ref, carry_ref, km_ref = refs[4 * pg:]
    j = pl.program_id(1)

    @pl.when(j == 0)
    def _():
        _attn_init(m_ref, l_ref, acc_ref)
        carry_ref[...] = jnp.zeros(carry_ref.shape, F32)

    _moba_gate_step(j, qb_ref, mk_refs, sel_ref, km_ref, pg=pg, nh=nh, page=page, n_blocks=n_blocks)

    qrows, own = _head_rows(q_ref[0], nh)
    q16 = qrows.astype(BF16)
    tri = (lax.broadcasted_iota(jnp.int32, (page, page), 0) <= lax.broadcasted_iota(jnp.int32, (page, page), 1))
    tri = tri.astype(F32)

    local = _nn(jnp.concatenate([_pad_rows(lf_refs[p][0]) for p in range(pg)], axis=0), tri, precision=HIGHEST)
    carry = carry_ref[...]
    scores = []
    for p in range(pg):
        loc = local[p * ROWS:(p + 1) * ROWS]
        scores.append(_nn(q16, k_refs[p][0].astype(BF16)) - (carry + loc))
        carry = carry + loc[:, page - 1:page]
    carry_ref[...] = carry
    s = jnp.concatenate(scores, axis=1)
    m_old = m_ref[...]
    m_new = jnp.maximum(m_old, jnp.max(s, axis=1, keepdims=True))
    alpha = jnp.exp(m_old - m_new)
    pr = jnp.exp(s - m_new)
    p16 = pr.astype(BF16)
    pv = _nt(p16[:, :page], v_refs[0][0].astype(BF16))
    for p in range(1, pg):
        pv = pv + _nt(p16[:, p * page:(p + 1) * page], v_refs[p][0].astype(BF16))
    l_ref[...] = alpha * l_ref[...] + jnp.sum(pr, axis=1, keepdims=True)
    acc_ref[...] = alpha * acc_ref[...] + pv
    m_ref[...] = m_new

    @pl.when(j == pl.num_programs(1) - 1)
    def _():
        lfn = _pad_rows(lfn_ref[0])
        s = jnp.sum(qrows * kn_ref[0], axis=1, keepdims=True) - (carry_ref[...] + lfn)
        m_old = m_ref[...]
        m_new = jnp.maximum(m_old, s)
        alpha = jnp.exp(m_old - m_new)
        p = jnp.exp(s - m_new)
        l = alpha * l_ref[...] + p
        acc = alpha * acc_ref[...] + p * vn_ref[0]
        o_ref[0] = jnp.sum(jnp.where(own, acc / l, 0.0), axis=0, keepdims=True)


def _decode_scan(pt, q, kn, vn, lfn, qb, kc, vc, lfc, mkc, *, pg):
    db, _, aw = q.shape
    nh = aw // HEAD_DIM
    page = kc.shape[2]
    npg = pt.shape[1]
    vec = pl.BlockSpec((1, 1, aw), lambda b, j, pt: (b, 0, 0))
    pspec = lambda shape, p: pl.BlockSpec(shape, lambda b, j, pt: (pt[b, j * pg + p], 0, 0))
    in_specs = [vec, vec, vec, pl.BlockSpec((1, nh, 1), lambda b, j, pt: (b, 0, 0)), vec]
    in_specs += [pspec((1, aw, page), p) for p in range(pg)] * 2
    in_specs += [pspec((1, nh, page), p) for p in range(pg)]
    in_specs += [pspec((1, aw, page), p) for p in range(pg)]
    return pl.pallas_call(
        functools.partial(_decode_scan_kernel, pg=pg, nh=nh, page=page, n_blocks=npg * page // MOBA_BLOCK),
        grid_spec=pltpu.PrefetchScalarGridSpec(
            num_scalar_prefetch=1, grid=(db, npg // pg), in_specs=in_specs,
            out_specs=[vec, pl.BlockSpec((1, nh, LANES), lambda b, j, pt: (b, 0, 0))],
            scratch_shapes=[pltpu.VMEM((ROWS, 1), F32), pltpu.VMEM((ROWS, 1), F32), pltpu.VMEM((ROWS, aw), F32),
                            pltpu.VMEM((ROWS, 1), F32), pltpu.VMEM((aw, LANES), F32)]),
        out_shape=[jax.ShapeDtypeStruct((db, 1, aw), F32), jax.ShapeDtypeStruct((db, nh, LANES), jnp.int32)],
        compiler_params=_params(("arbitrary", "arbitrary")),
        name="decode_scan",
    )(pt, q, kn, vn, lfn, qb, *([kc] * pg), *([vc] * pg), *([lfc] * pg), *([mkc] * pg))


def _moba_gate_step(j, q_ref, k_refs, sel_ref, km_ref, *, pg, nh, page, n_blocks):
    per_block = MOBA_BLOCK // page

    @pl.when(j == 0)
    def _():
        km_ref[...] = jnp.zeros(km_ref.shape, F32)

    km = km_ref[...]
    lane = lax.broadcasted_iota(jnp.int32, km.shape, 1)
    for g in range(pg // per_block):
        tot = k_refs[g * per_block][0]
        for u in range(1, per_block):
            tot = tot + k_refs[g * per_block + u][0]
        mean = jnp.sum(tot, axis=1, keepdims=True) * (1.0 / MOBA_BLOCK)
        km = jnp.where(lane == j * (pg // per_block) + g, mean, km)
    km_ref[...] = km

    @pl.when(j == pl.num_programs(1) - 1)
    def _():
        qrows, _ = _head_rows(q_ref[0], nh)
        gate = _nn(qrows, km_ref[...], precision=HIGHEST)
        lane = lax.broadcasted_iota(jnp.int32, gate.shape, 1)
        g = jnp.where(lane < n_blocks, gate, NEG_INF)
        lane_f = lane.astype(F32)
        out = jnp.zeros(gate.shape, F32)
        for t in range(MOBA_TOPK):
            m = jnp.max(g, axis=1, keepdims=True)
            idx = jnp.min(jnp.where(g == m, lane_f, 1e9), axis=1, keepdims=True)
            out = jnp.where(lane == t, idx, out)
            g = jnp.where(lane_f == idx, -3e38, g)
        sel_ref[0] = out[:nh].astype(jnp.int32)


def _moba_decode_kernel(pt_ref, sel_ref, q_ref, kn_ref, vn_ref, *refs, n_pages, hs):
    k_refs, v_refs, o_ref = refs[:hs * n_pages], refs[hs * n_pages:2 * hs * n_pages], refs[2 * hs * n_pages]
    outs = []
    for h in range(hs):
        q, kn, vn = q_ref[0, 0, h:h + 1, :], kn_ref[0, 0, h:h + 1, :], vn_ref[0, 0, h:h + 1, :]
        q16 = jnp.broadcast_to(q, (ROWS, HEAD_DIM)).astype(BF16)
        s_new = jnp.sum(q * kn, axis=1, keepdims=True)
        ss = [_nn(q16, k_refs[h * n_pages + r][0].astype(BF16)) for r in range(n_pages)]
        m = s_new
        for s in ss:
            m = jnp.maximum(m, jnp.max(s, axis=1, keepdims=True))
        p_new = jnp.exp(s_new - m)
        l = p_new
        acc = p_new * vn
        for r, s in enumerate(ss):
            p = jnp.exp(s - m)
            l = l + jnp.sum(p, axis=1, keepdims=True)
            acc = acc + _nt(p.astype(BF16), v_refs[h * n_pages + r][0].astype(BF16))
        outs.append((acc / l)[0:1])
    o_ref[0, 0] = jnp.concatenate(outs, axis=0)


def _moba_decode(pt, sel, q, kn, vn, kc, vc, *, hs=2):
    db, ng, _, hd = q.shape
    page = kc.shape[2]
    per_block = MOBA_BLOCK // page
    n_pages = MOBA_TOPK * per_block
    vec = pl.BlockSpec((1, 1, hs, hd), lambda b, g, pt, sel: (b, g, 0, 0))

    def pspec(e, r):
        t, u = divmod(r, per_block)
        return pl.BlockSpec(
            (1, hd, page),
            lambda b, g, pt, sel: (pt[b, sel[b, (g * hs + e) * MOBA_TOPK + t] * per_block + u], g * hs + e, 0))

    pages = [pspec(e, r) for e in range(hs) for r in range(n_pages)]
    return pl.pallas_call(
        functools.partial(_moba_decode_kernel, n_pages=n_pages, hs=hs),
        grid_spec=pltpu.PrefetchScalarGridSpec(
            num_scalar_prefetch=2, grid=(db, ng), in_specs=[vec, vec, vec] + pages + pages, out_specs=vec),
        out_shape=jax.ShapeDtypeStruct((db, ng, hs, hd), F32),
        compiler_params=_params(("arbitrary", "arbitrary")),
        name="moba_decode",
    )(pt, sel, q, kn, vn, *([kc] * (hs * n_pages)), *([vc] * (hs * n_pages)))


def _rope_tables(pos):
    inv = ROPE_THETA ** (-jnp.arange(HALF, dtype=F32) / HALF)
    ang = pos.astype(F32)[:, None] * inv[None, :]
    cos, sin, zero = jnp.cos(ang), jnp.sin(ang), jnp.zeros_like(ang)
    reps = LANES // HEAD_DIM
    return (jnp.tile(cos, (1, 2 * reps)), jnp.tile(jnp.concatenate([-sin, zero], axis=1), (1, reps)),
            jnp.tile(jnp.concatenate([zero, sin], axis=1), (1, reps)))


def _pick(n, candidates):
    for c in candidates:
        if n % c == 0:
            return c
    raise ValueError(f"no tile for {n}")


def kernel(x_prompt, x_sample, cache_fox_k, cache_fox_v, cache_fox_logf, cache_moba_k, cache_moba_v, state_ffn_conv, page_table, norm_mix_g, w_in, b_forget, w_fox_o, w_moba_o, w_out, norm_ffn_g, w_up, conv_w, conv_b, w_down, norm_final_g):
    batch, seq, d = x_prompt.shape
    db, dseq, _ = x_sample.shape
    depth, n_pool, page, nh, hd = cache_fox_k.shape
    aw = nh * hd
    dff = w_down.shape[1]
    n_pages = page_table.shape[1]
    past = n_pages * page
    assert depth == 1 and hd == HEAD_DIM and cache_moba_k.shape == cache_fox_k.shape and dseq == 1
    assert seq // MOBA_BLOCK <= HEAD_DIM and past // MOBA_BLOCK <= LANES
    assert seq % MOBA_BLOCK == 0 and past % MOBA_BLOCK == 0 and past // MOBA_BLOCK >= MOBA_TOPK
    assert MOBA_BLOCK % page == 0 and aw % LANES == 0 and nh <= 8

    n = batch * seq
    tm = _pick(seq, (512, 256, 128))
    t_attn = _pick(seq, (1024, 512, MOBA_BLOCK))
    tf = _pick(dff, (1408, 1024, 512, 256, 128))
    pg = _pick(n_pages, (8, 4, 2))
    tables_p = _rope_tables(jnp.arange(seq))
    tables_s = _rope_tables(jnp.full((db,), past))

    xp = x_prompt.reshape(n, d)
    xs = x_sample.reshape(db, d)
    feat_major = lambda c: jnp.transpose(c, (0, 1, 3, 4, 2)).reshape(depth * n_pool, aw, page)
    fk, fv, mk, mv = (feat_major(c) for c in (cache_fox_k, cache_fox_v, cache_moba_k, cache_moba_v))
    fl = jnp.transpose(cache_fox_logf, (0, 1, 3, 2)).reshape(depth * n_pool, nh, page)
    pos_major = lambda t: jnp.transpose(t.reshape(batch, nh, -1, seq), (0, 3, 1, 2))
    sizes = (aw, aw, aw, nh, aw, aw, aw, d, d)
    offs = [sum(sizes[:i]) for i in range(len(sizes) + 1)]

    p_new, s_new = [], []
    for l in range(depth):
        cols = [w_in[l][:, offs[i]:offs[i + 1]] for i in range(len(sizes))]
        qa, ka, va, fa, qb, kb, vb, ga, gb = cols
        w_cat = jnp.concatenate([qa * ATTN_SCALE, ka, va, qb * ATTN_SCALE, kb, vb, ga, gb,
                                 jnp.pad(fa, ((0, 0), (0, LANES - nh)))], axis=1).astype(BF16)
        bf = jnp.pad(b_forget[l].astype(F32), (0, LANES - nh)).reshape(1, LANES)
        g_mix = norm_mix_g[l].reshape(1, d)
        g_ffn = norm_ffn_g[l].reshape(1, d)
        wfo, wmo, wout = w_fox_o[l].astype(BF16), w_moba_o[l].astype(BF16), w_out[l].astype(BF16)
        wup, wd = w_up[l].astype(BF16), w_down[l].astype(BF16)
        cw, cb = conv_w[l], conv_b[l].reshape(1, 2 * dff)
        gf = norm_final_g.reshape(1, d)
        pt = page_table + l * n_pool

        qa_p, ka_p, va_p, qb_p, kb_p, vb_p, ga_p, gb_p, lf_p = _inproj(
            xp, g_mix, w_cat, bf, *tables_p, tm=tm, aw=aw, nh=nh, seq=seq)
        oa_p = _fox_prompt(qa_p, ka_p, va_p, lf_p, batch=batch, seq=seq, t=t_attn)
        ob_p = _moba_prompt(qb_p, kb_p, vb_p, batch=batch, seq=seq, t=t_attn)
        x1_p, h2_p = _merge(oa_p, ob_p, ga_p, gb_p, xp, wfo, wmo, wout, g_ffn, tm=tm)
        xp, tails = _ffn_prompt(h2_p, x1_p, wup, cw, cb, wd, gf, seq=seq, tm=tm, sw=_pick(dff, (256, LANES)))
        tps = seq // tm
        tail_p = tails[tps - 1::tps, 8 - (CONV_W - 1):]
        p_new.append((pos_major(ka_p), pos_major(va_p), jnp.transpose(lf_p, (0, 2, 1)), pos_major(kb_p),
                      pos_major(vb_p), tail_p))

        qa_s, ka_s, va_s, qb_s, kb_s, vb_s, ga_s, gb_s, lf_s = _inproj(
            xs, g_mix, w_cat, bf, *tables_s, tm=db, aw=aw, nh=nh)
        r1 = lambda t: t.reshape(db, 1, aw)
        r3 = lambda t: t.reshape(db, nh // 2, 2, hd)
        oa_s, sel = _decode_scan(pt, r1(qa_s), r1(ka_s), r1(va_s), lf_s.reshape(db, nh, 1), r1(qb_s),
                                 fk, fv, fl, mk, pg=pg)
        sel = sel[:, :, :MOBA_TOPK].reshape(db, nh * MOBA_TOPK)
        ob_s = _moba_decode(pt, sel, r3(qb_s), r3(kb_s), r3(vb_s), mk, mv)
        x1_s, h2_s = _merge(oa_s.reshape(db, aw), ob_s.reshape(db, aw), ga_s, gb_s, xs, wfo, wmo, wout, g_ffn, tm=db)
        prev = state_ffn_conv[l]
        xs, ua, ug = _ffn_sample(h2_s, x1_s, prev.reshape(db, (CONV_W - 1) * 2 * dff), wup, cw, cb, wd, gf, tf=tf)
        tail_s = jnp.concatenate([prev[:, 1:], jnp.concatenate([ua, ug], axis=-1)[:, None]], axis=1)
        s_new.append((ka_s.reshape(db, 1, nh, hd), va_s.reshape(db, 1, nh, hd), lf_s.reshape(db, 1, nh),
                      kb_s.reshape(db, 1, nh, hd), vb_s.reshape(db, 1, nh, hd), tail_s))

    p_out = [jnp.stack(z) for z in zip(*p_new)]
    s_out = [jnp.stack(z) for z in zip(*s_new)]
    return (xp.reshape(batch, seq, d), xs.reshape(db, 1, d), *p_out, *s_out)
```

```python
import functools

import jax
import jax.numpy as jnp
from jax import lax
from jax.experimental import pallas as pl
from jax.experimental.pallas import tpu as pltpu

HEAD_DIM = 64
HALF = HEAD_DIM // 2
LANES = 128
MOBA_BLOCK = 256
MOBA_TOPK = 3
CONV_W = 3
ROPE_THETA = 10000.0
RMS_EPS = 1e-6
NEG_INF = -1e30
MASK_BIAS = -2e30
ATTN_SCALE = HEAD_DIM ** -0.5
LOG2E = 1.4426950408889634
VMEM_LIMIT = 56 * 1024 * 1024

F32 = jnp.float32
BF16 = jnp.bfloat16
HIGHEST = lax.Precision.HIGHEST


def _nt(a, b, precision=None):
    return lax.dot_general(a, b, (((1,), (1,)), ((), ())), preferred_element_type=F32, precision=precision)


def _nn(a, b, precision=None):
    return jnp.dot(a, b, preferred_element_type=F32, precision=precision)


def _params(sem):
    return pltpu.CompilerParams(dimension_semantics=sem, vmem_limit_bytes=VMEM_LIMIT)


def _resident(shape):
    nd = len(shape)
    return pl.BlockSpec(shape, lambda *_: (0,) * nd, pipeline_mode=pl.Buffered(1))


def _inproj_kernel(x_ref, g_ref, w_ref, bf_ref, cos_ref, sina_ref, sinb_ref,
                   qa_ref, ka_ref, va_ref, qb_ref, kb_ref, vb_ref, ga_ref, gb_ref, lf_ref,
                   *, aw, d_model, nh, position_minor):
    x = x_ref[...]
    h = (x * lax.rsqrt(jnp.mean(x * x, axis=-1, keepdims=True) + RMS_EPS) * g_ref[...]).astype(BF16)

    def proj(c0, width):
        return _nn(h, w_ref[:, c0:c0 + width])

    def put(o_ref, val):
        if position_minor:
            o_ref[0] = val.T
        else:
            o_ref[...] = val

    cos, sina, sinb = cos_ref[...], sina_ref[...], sinb_ref[...]

    def rope(t):
        out = []
        for s in range(aw // LANES):
            xs = t[:, s * LANES:(s + 1) * LANES]
            out.append(xs * cos + pltpu.roll(xs, LANES - HALF, 1) * sina + pltpu.roll(xs, HALF, 1) * sinb)
        return jnp.concatenate(out, axis=1)

    qa_ref[...] = proj(0, aw)
    put(ka_ref, proj(aw, aw))
    put(va_ref, proj(2 * aw, aw))
    qb_ref[...] = rope(proj(3 * aw, aw))
    put(kb_ref, rope(proj(4 * aw, aw)))
    put(vb_ref, proj(5 * aw, aw))
    ga_ref[...] = jax.nn.sigmoid(proj(6 * aw, d_model)).astype(ga_ref.dtype)
    gb_ref[...] = jax.nn.sigmoid(proj(6 * aw + d_model, d_model)).astype(gb_ref.dtype)

    f = proj(6 * aw + 2 * d_model, LANES) + bf_ref[...]
    lsig = -(jnp.maximum(-f, 0.0) + jnp.log1p(jnp.exp(-jnp.abs(f))))
    if position_minor:
        lf_ref[0] = lsig.T[:nh, :]
    else:
        lf_ref[...] = lsig[:, :nh]


def _inproj(x, g, w, bf, cos, sina, sinb, *, tm, aw, nh, seq=None):
    n, d = x.shape
    nt = n // tm
    tb = cos.shape[0] // tm
    row = lambda i: (i, 0)
    tbl = lambda i: (i % tb, 0)
    nat = (jax.ShapeDtypeStruct((n, aw), F32), pl.BlockSpec((tm, aw), row))
    gate = (jax.ShapeDtypeStruct((n, d), BF16), pl.BlockSpec((tm, d), row))
    if seq is None:
        kv = nat
        lf = (jax.ShapeDtypeStruct((n, nh), F32), pl.BlockSpec((tm, nh), row))
    else:
        tps = seq // tm
        pm = lambda i: (i // tps, 0, i % tps)
        kv = (jax.ShapeDtypeStruct((n // seq, aw, seq), F32), pl.BlockSpec((1, aw, tm), pm))
        lf = (jax.ShapeDtypeStruct((n // seq, nh, seq), F32), pl.BlockSpec((1, nh, tm), pm))
    outs = [nat, kv, kv, nat, kv, kv, gate, gate, lf]
    return pl.pallas_call(
        functools.partial(_inproj_kernel, aw=aw, d_model=d, nh=nh, position_minor=seq is not None),
        grid=(nt,),
        in_specs=[pl.BlockSpec((tm, d), row), _resident(g.shape), _resident(w.shape), _resident(bf.shape),
                  pl.BlockSpec((tm, LANES), tbl), pl.BlockSpec((tm, LANES), tbl), pl.BlockSpec((tm, LANES), tbl)],
        out_specs=[o[1] for o in outs],
        out_shape=[o[0] for o in outs],
        compiler_params=_params(("arbitrary",)),
        name="inproj",
    )(x, g, w, bf, cos, sina, sinb)


def _attn_init(m_ref, l_ref, acc_ref):
    m_ref[...] = jnp.full(m_ref.shape, NEG_INF, F32)
    l_ref[...] = jnp.zeros(l_ref.shape, F32)
    acc_ref[...] = jnp.zeros(acc_ref.shape, F32)


def _flash_causal(i, qts, k_refs, vt_ref, o_ref, *, t):
    def block(j, carry, masked=False):
        off = pl.multiple_of(j * t, t)
        out = []
        for e in range(2):
            m_old, l_old, acc = carry[e]
            s = _nn(k_refs[e][pl.ds(off, t), :], qts[e])
            if masked:
                kpos = lax.broadcasted_iota(jnp.int32, (t, t), 0)
                qpos = lax.broadcasted_iota(jnp.int32, (t, t), 1)
                s = jnp.where(kpos <= qpos, s, NEG_INF)
            m_new = jnp.maximum(m_old, jnp.max(s, axis=0, keepdims=True))
            alpha = jnp.exp2(m_old - m_new)
            p = jnp.exp2(s - m_new)
            l_new = alpha * l_old + jnp.sum(p, axis=0, keepdims=True)
            vt = vt_ref[e * HEAD_DIM:(e + 1) * HEAD_DIM, pl.ds(off, t)]
            out.append((m_new, l_new, alpha * acc + _nn(vt, p.astype(BF16))))
        return tuple(out)

    init = tuple((jnp.full((1, t), NEG_INF, F32), jnp.zeros((1, t), F32), jnp.zeros((HEAD_DIM, t), F32))
                 for _ in range(2))
    carry = lax.fori_loop(0, i // 2, lambda jj, c: block(2 * jj + 1, block(2 * jj, c)), init)
    carry = lax.cond(i % 2 == 1, lambda c: block(i - 1, c), lambda c: c, carry)
    carry = block(i, carry, masked=True)
    o_ref[...] = jnp.concatenate([acc / l for _, l, acc in carry], axis=0).T.astype(o_ref.dtype)


def _split3(x):
    hi = x.astype(BF16).astype(F32)
    mid = (x - hi).astype(BF16).astype(F32)
    lo = (x - hi - mid).astype(BF16).astype(F32)
    return hi, mid, lo


def _fox_prompt_kernel(q_ref, k_ref, v_ref, lft_ref, o_ref, ka16, kb16, vt16, *, t, seq):
    i = pl.program_id(2)

    @pl.when(i == 0)
    def _():
        c = lft_ref[0, 0]
        pos = lax.broadcasted_iota(jnp.int32, c.shape, 1)
        sh = 1
        while sh < seq:
            c = c + jnp.where(pos >= sh, pltpu.roll(c, sh, 1), 0.0)
            sh *= 2
        sub = lax.broadcasted_iota(jnp.int32, (8, seq), 0)
        rows = []
        for e in range(2):
            parts = _split3(-LOG2E * c[e:e + 1])
            rows.append(sum(jnp.where(sub == r, jnp.broadcast_to(part, (8, seq)), 0.0)
                            for r, part in enumerate(parts)))
        fill = jnp.zeros((HEAD_DIM - 8, seq), F32)
        decay = jnp.concatenate([rows[1], fill, rows[0], fill], axis=0).T
        k = k_ref[0].T
        lane = lax.broadcasted_iota(jnp.int32, k.shape, 1)
        ka16[...] = jnp.where(lane < HEAD_DIM, k, decay).astype(BF16)
        kb16[...] = jnp.where(lane >= HEAD_DIM, k, decay).astype(BF16)
        vt16[...] = v_ref[0].astype(BF16)

    qt = (q_ref[...] * LOG2E).T
    ones = jnp.where(lax.broadcasted_iota(jnp.int32, (8, t), 0) < 3, 1.0, 0.0)
    fill = jnp.zeros((HEAD_DIM - 8, t), F32)
    qts = (jnp.concatenate([qt[:HEAD_DIM], ones, fill], axis=0).astype(BF16),
           jnp.concatenate([ones, fill, qt[HEAD_DIM:]], axis=0).astype(BF16))
    _flash_causal(i, qts, (ka16, kb16), vt16, o_ref, t=t)


def _fox_prompt(q, k, v, lft, *, batch, seq, t):
    n, aw = q.shape
    hp = aw // LANES
    nq = seq // t
    lft4 = lft.reshape(batch, hp, 2, seq)
    qmap = lambda b, h, i: (b * nq + i, h)
    kmap = lambda b, h, i: (b, h, 0)
    return pl.pallas_call(
        functools.partial(_fox_prompt_kernel, t=t, seq=seq),
        grid=(batch, hp, nq),
        in_specs=[pl.BlockSpec((t, LANES), qmap), pl.BlockSpec((1, LANES, seq), kmap), pl.BlockSpec((1, LANES, seq), kmap),
                  pl.BlockSpec((1, 1, 2, seq), lambda b, h, i: (b, h, 0, 0))],
        out_specs=pl.BlockSpec((t, LANES), qmap),
        out_shape=jax.ShapeDtypeStruct((n, aw), BF16),
        scratch_shapes=[pltpu.VMEM((seq, LANES), BF16), pltpu.VMEM((seq, LANES), BF16), pltpu.VMEM((LANES, seq), BF16)],
        compiler_params=_params(("arbitrary", "arbitrary", "arbitrary")),
        name="fox_prompt",
    )(q, k, v, lft4)


def _block_bias(gate, limit):
    n = lax.broadcasted_iota(jnp.int32, gate.shape, 0)
    nf = n.astype(F32)
    g = jnp.where(n < limit, gate, NEG_INF)
    sel = jnp.zeros(gate.shape, F32)
    for _ in range(MOBA_TOPK):
        m = jnp.max(g, axis=0, keepdims=True)
        idx = jnp.min(jnp.where(g == m, nf, 1e9), axis=0, keepdims=True)
        hit = nf == idx
        sel = jnp.where(hit & (idx < limit), 1.0, sel)
        g = jnp.where(hit, -3e38, g)
    return jnp.where((n < limit) & (sel < 0.5), MASK_BIAS, 0.0)


def _moba_prompt_kernel(q_ref, k_ref, v_ref, o_ref, ka16, kb16, vt16, km_ref, *, seq, nbp, t):
    bs = MOBA_BLOCK
    nb = seq // bs
    i = pl.program_id(2)

    @pl.when(i == 0)
    def _():
        k = k_ref[0].T
        lane = lax.broadcasted_iota(jnp.int32, k.shape, 1)
        blk = lax.broadcasted_iota(jnp.int32, k.shape, 0) // bs
        ka16[...] = jnp.where(lane < HEAD_DIM, k, jnp.where(lane - HEAD_DIM == blk, 1.0, 0.0)).astype(BF16)
        kb16[...] = jnp.where(lane >= HEAD_DIM, k, jnp.where(lane == blk, 1.0, 0.0)).astype(BF16)
        vt16[...] = v_ref[0].astype(BF16)
        km_ref[...] = jnp.zeros(km_ref.shape, F32)
        for n in range(nb):
            km_ref[n:n + 1, :] = jnp.sum(k[n * bs:(n + 1) * bs, :], axis=0, keepdims=True) * (1.0 / bs)

    qt = q_ref[...].T
    km = km_ref[...]
    lane = lax.broadcasted_iota(jnp.int32, km.shape, 1)
    fill = jnp.zeros((HEAD_DIM - nbp, t), F32)
    own = (i * t + lax.broadcasted_iota(jnp.int32, (1, t), 1)) // bs
    bias_a = _block_bias(_nn(jnp.where(lane < HEAD_DIM, km, 0.0), qt, precision=HIGHEST), own)
    bias_b = _block_bias(_nn(jnp.where(lane >= HEAD_DIM, km, 0.0), qt, precision=HIGHEST), own)
    qs = qt * LOG2E
    qts = (jnp.concatenate([qs[:HEAD_DIM], bias_a, fill], axis=0).astype(BF16),
           jnp.concatenate([bias_b, fill, qs[HEAD_DIM:]], axis=0).astype(BF16))
    _flash_causal(i, qts, (ka16, kb16), vt16, o_ref, t=t)


def _moba_prompt(q, k, v, *, batch, seq, t):
    n, aw = q.shape
    hp = aw // LANES
    nq = seq // t
    nbp = -(-(seq // MOBA_BLOCK) // 8) * 8
    qmap = lambda b, h, i: (b * nq + i, h)
    kmap = lambda b, h, i: (b, h, 0)
    return pl.pallas_call(
        functools.partial(_moba_prompt_kernel, seq=seq, nbp=nbp, t=t),
        grid=(batch, hp, nq),
        in_specs=[pl.BlockSpec((t, LANES), qmap), pl.BlockSpec((1, LANES, seq), kmap), pl.BlockSpec((1, LANES, seq), kmap)],
        out_specs=pl.BlockSpec((t, LANES), qmap),
        out_shape=jax.ShapeDtypeStruct((n, aw), BF16),
        scratch_shapes=[pltpu.VMEM((seq, LANES), BF16), pltpu.VMEM((seq, LANES), BF16), pltpu.VMEM((LANES, seq), BF16),
                        pltpu.VMEM((nbp, LANES), F32)],
        compiler_params=_params(("arbitrary", "arbitrary", "arbitrary")),
        name="moba_prompt",
    )(q, k, v)


def _merge_kernel(oa_ref, ob_ref, ga_ref, gb_ref, x_ref, wfo_ref, wmo_ref, wout_ref, g_ref, x1_ref, h_ref):
    pa = _nn(oa_ref[...].astype(BF16), wfo_ref[...])
    pb = _nn(ob_ref[...].astype(BF16), wmo_ref[...])
    mix = ga_ref[...] * pa + gb_ref[...] * pb
    x1 = x_ref[...] + _nn(mix.astype(BF16), wout_ref[...])
    x1_ref[...] = x1
    h_ref[...] = (x1 * lax.rsqrt(jnp.mean(x1 * x1, axis=-1, keepdims=True) + RMS_EPS) * g_ref[...]).astype(BF16)


def _merge(oa, ob, ga, gb, x, wfo, wmo, wout, g, *, tm):
    n, d = x.shape
    aw = oa.shape[1]
    row = lambda i: (i, 0)
    return pl.pallas_call(
        _merge_kernel,
        grid=(n // tm,),
        in_specs=[pl.BlockSpec((tm, aw), row), pl.BlockSpec((tm, aw), row), pl.BlockSpec((tm, d), row),
                  pl.BlockSpec((tm, d), row), pl.BlockSpec((tm, d), row),
                  _resident(wfo.shape), _resident(wmo.shape), _resident(wout.shape), _resident(g.shape)],
        out_specs=[pl.BlockSpec((tm, d), row), pl.BlockSpec((tm, d), row)],
        out_shape=[jax.ShapeDtypeStruct((n, d), F32), jax.ShapeDtypeStruct((n, d), BF16)],
        compiler_params=_params(("arbitrary",)),
        name="merge",
    )(oa, ob, ga, gb, x, wfo, wmo, wout, g)


def _gated_act(ca, cg):
    gelu = 0.5 * ca * (1.0 + lax.erf(ca * (2.0 ** -0.5)))
    return (gelu * cg).astype(BF16)


def _ffn_prompt_kernel(h_ref, x1_ref, wup_ref, cw_ref, cb_ref, wd_ref, gf_ref, y_ref, tail_ref, prev_ref,
                       *, tiles_per_seq, dff, sw):
    i = pl.program_id(0)
    h = h_ref[...]
    tm = h.shape[0]
    row = lax.broadcasted_iota(jnp.int32, (8, sw), 0)

    @pl.when(i % tiles_per_seq == 0)
    def _():
        prev_ref[...] = jnp.zeros(prev_ref.shape, F32)

    def conv(c0):
        u = _nn(h, wup_ref[:, c0:c0 + sw])
        last = u[tm - 8:, :]
        tail_ref[0, :, c0:c0 + sw] = last
        prev = prev_ref[:, c0:c0 + sw]
        prev_ref[:, c0:c0 + sw] = last
        s1 = pltpu.roll(u, 1, 0)
        s2 = pltpu.roll(u, 2, 0)
        s1 = jnp.concatenate([jnp.where(row < 1, pltpu.roll(prev, 1, 0), s1[:8]), s1[8:]], axis=0)
        s2 = jnp.concatenate([jnp.where(row < 2, pltpu.roll(prev, 2, 0), s2[:8]), s2[8:]], axis=0)
        cw = cw_ref[:, c0:c0 + sw]
        return cb_ref[:, c0:c0 + sw] + cw[0:1] * s2 + cw[1:2] * s1 + cw[2:3] * u

    acts = [_gated_act(conv(c0), conv(dff + c0)) for c0 in range(0, dff, sw)]
    xo = x1_ref[...] + _nn(jnp.concatenate(acts, axis=1), wd_ref[...])
    y_ref[...] = xo * lax.rsqrt(jnp.mean(xo * xo, axis=-1, keepdims=True) + RMS_EPS) * gf_ref[...]


def _ffn_prompt(h, x1, wup, cw, cb, wd, gf, *, seq, tm, sw):
    n, d = x1.shape
    dff = wd.shape[0]
    row = lambda i: (i, 0)
    return pl.pallas_call(
        functools.partial(_ffn_prompt_kernel, tiles_per_seq=seq // tm, dff=dff, sw=sw),
        grid=(n // tm,),
        in_specs=[pl.BlockSpec((tm, d), row), pl.BlockSpec((tm, d), row), _resident(wup.shape), _resident(cw.shape),
                  _resident(cb.shape), _resident(wd.shape), _resident(gf.shape)],
        out_specs=[pl.BlockSpec((tm, d), row), pl.BlockSpec((1, 8, 2 * dff), lambda i: (i, 0, 0))],
        out_shape=[jax.ShapeDtypeStruct((n, d), F32), jax.ShapeDtypeStruct((n // tm, 8, 2 * dff), F32)],
        scratch_shapes=[pltpu.VMEM((8, 2 * dff), F32)],
        compiler_params=_params(("arbitrary",)),
        name="ffn_prompt",
    )(h, x1, wup, cw, cb, wd, gf)


def _ffn_sample_kernel(h_ref, x1_ref, p0a_ref, p0g_ref, p1a_ref, p1g_ref, wa_ref, wg_ref, cwa_ref, cwg_ref,
                       cba_ref, cbg_ref, wd_ref, gf_ref, y_ref, ua_ref, ug_ref, acc_ref, *, n_chunks):
    c = pl.program_id(0)
    h = h_ref[...]

    def conv(w_ref, cw_ref, cb_ref, p0_ref, p1_ref, u_ref):
        u = _nn(h, w_ref[...])
        u_ref[...] = u
        cw = cw_ref[...]
        return cb_ref[...] + cw[0:1] * p0_ref[...] + cw[1:2] * p1_ref[...] + cw[2:3] * u

    ca = conv(wa_ref, cwa_ref, cba_ref, p0a_ref, p1a_ref, ua_ref)
    cg = conv(wg_ref, cwg_ref, cbg_ref, p0g_ref, p1g_ref, ug_ref)
    part = _nn(_gated_act(ca, cg), wd_ref[...])

    @pl.when(c == 0)
    def _():
        acc_ref[...] = part

    @pl.when(c > 0)
    def _():
        acc_ref[...] += part

    @pl.when(c == n_chunks - 1)
    def _():
        xo = x1_ref[...] + acc_ref[...]
        y_ref[...] = xo * lax.rsqrt(jnp.mean(xo * xo, axis=-1, keepdims=True) + RMS_EPS) * gf_ref[...]


def _ffn_sample(h, x1, prev, wup, cw, cb, wd, gf, *, tf):
    n, d = x1.shape
    dff = wd.shape[0]
    nc = dff // tf
    full = lambda c: (0, 0)
    col = lambda k: (lambda c: (0, k * nc + c))
    return pl.pallas_call(
        functools.partial(_ffn_sample_kernel, n_chunks=nc),
        grid=(nc,),
        in_specs=[pl.BlockSpec((n, d), full), pl.BlockSpec((n, d), full),
                  pl.BlockSpec((n, tf), col(0)), pl.BlockSpec((n, tf), col(1)),
                  pl.BlockSpec((n, tf), col(2)), pl.BlockSpec((n, tf), col(3)),
                  pl.BlockSpec((d, tf), col(0)), pl.BlockSpec((d, tf), col(1)),
                  pl.BlockSpec((CONV_W, tf), col(0)), pl.BlockSpec((CONV_W, tf), col(1)),
                  pl.BlockSpec((1, tf), col(0)), pl.BlockSpec((1, tf), col(1)),
                  pl.BlockSpec((tf, d), lambda c: (c, 0)), pl.BlockSpec((1, d), full)],
        out_specs=[pl.BlockSpec((n, d), full), pl.BlockSpec((n, tf), col(0)), pl.BlockSpec((n, tf), col(0))],
        out_shape=[jax.ShapeDtypeStruct((n, d), F32), jax.ShapeDtypeStruct((n, dff), F32),
                   jax.ShapeDtypeStruct((n, dff), F32)],
        scratch_shapes=[pltpu.VMEM((n, d), F32)],
        compiler_params=_params(("arbitrary",)),
        name="ffn_sample",
    )(h, x1, prev, prev, prev, prev, wup, wup, cw, cw, cb, cb, wd, gf)


ROWS = 16


def _pad_rows(x):
    return jnp.concatenate([x, jnp.zeros((ROWS - x.shape[0],) + x.shape[1:], x.dtype)], axis=0)


def _head_rows(q, nh):
    shape = (ROWS, q.shape[1])
    sub = lax.broadcasted_iota(jnp.int32, shape, 0)
    lane = lax.broadcasted_iota(jnp.int32, shape, 1)
    own = lane // HEAD_DIM == sub
    return jnp.where(own, jnp.broadcast_to(q, shape), 0.0), own


def _decode_scan_kernel(pt_ref, q_ref, kn_ref, vn_ref, lfn_ref, qb_ref, *refs, pg, nh, page, n_blocks):
    k_refs, v_refs, lf_refs, mk_refs = (refs[r * pg:(r + 1) * pg] for r in range(4))
    o_ref, sel_ref, m_ref, l_ref, acc_ref, carry_ref, km_ref = refs[4 * pg:]
    j = pl.program_id(1)

    @pl.when(j == 0)
    def _():
        _attn_init(m_ref, l_ref, acc_ref)
        carry_ref[...] = jnp.zeros(carry_ref.shape, F32)

    _moba_gate_step(j, qb_ref, mk_refs, sel_ref, km_ref, pg=pg, nh=nh, page=page, n_blocks=n_blocks)

    qrows, own = _head_rows(q_ref[0], nh)
    q16 = qrows.astype(BF16)
    tri = (lax.broadcasted_iota(jnp.int32, (page, page), 0) <= lax.broadcasted_iota(jnp.int32, (page, page), 1))
    tri = tri.astype(F32)

    local = _nn(jnp.concatenate([_pad_rows(lf_refs[p][0]) for p in range(pg)], axis=0), tri, precision=HIGHEST)
    carry = carry_ref[...]
    scores = []
    for p in range(pg):
        loc = local[p * ROWS:(p + 1) * ROWS]
        scores.append(_nn(q16, k_refs[p][0].astype(BF16)) - (carry + loc))
        carry = carry + loc[:, page - 1:page]
    carry_ref[...] = carry
    s = jnp.concatenate(scores, axis=1)
    m_old = m_ref[...]
    m_new = jnp.maximum(m_old, jnp.max(s, axis=1, keepdims=True))
    alpha = jnp.exp(m_old - m_new)
    pr = jnp.exp(s - m_new)
    p16 = pr.astype(BF16)
    pv = _nt(p16[:, :page], v_refs[0][0].astype(BF16))
    for p in range(1, pg):
        pv = pv + _nt(p16[:, p * page:(p + 1) * page], v_refs[p][0].astype(BF16))
    l_ref[...] = alpha * l_ref[...] + jnp.sum(pr, axis=1, keepdims=True)
    acc_ref[...] = alpha * acc_ref[...] + pv
    m_ref[...] = m_new

    @pl.when(j == pl.num_programs(1) - 1)
    def _():
        lfn = _pad_rows(lfn_ref[0])
        s = jnp.sum(qrows * kn_ref[0], axis=1, keepdims=True) - (carry_ref[...] + lfn)
        m_old = m_ref[...]
        m_new = jnp.maximum(m_old, s)
        alpha = jnp.exp(m_old - m_new)
        p = jnp.exp(s - m_new)
        l = alpha * l_ref[...] + p
        acc = alpha * acc_ref[...] + p * vn_ref[0]
        o_ref[0] = jnp.sum(jnp.where(own, acc / l, 0.0), axis=0, keepdims=True)


def _decode_scan(pt, q, kn, vn, lfn, qb, kc, vc, lfc, mkc, *, pg):
    db, _, aw = q.shape
    nh = aw // HEAD_DIM
    page = kc.shape[2]
    npg = pt.shape[1]
    vec = pl.BlockSpec((1, 1, aw), lambda b, j, pt: (b, 0, 0))
    pspec = lambda shape, p: pl.BlockSpec(shape, lambda b, j, pt: (pt[b, j * pg + p], 0, 0))
    in_specs = [vec, vec, vec, pl.BlockSpec((1, nh, 1), lambda b, j, pt: (b, 0, 0)), vec]
    in_specs += [pspec((1, aw, page), p) for p in range(pg)] * 2
    in_specs += [pspec((1, nh, page), p) for p in range(pg)]
    in_specs += [pspec((1, aw, page), p) for p in range(pg)]
    return pl.pallas_call(
        functools.partial(_decode_scan_kernel, pg=pg, nh=nh, page=page, n_blocks=npg * page // MOBA_BLOCK),
        grid_spec=pltpu.PrefetchScalarGridSpec(
            num_scalar_prefetch=1, grid=(db, npg // pg), in_specs=in_specs,
            out_specs=[vec, pl.BlockSpec((1, nh, LANES), lambda b, j, pt: (b, 0, 0))],
            scratch_shapes=[pltpu.VMEM((ROWS, 1), F32), pltpu.VMEM((ROWS, 1), F32), pltpu.VMEM((ROWS, aw), F32),
                            pltpu.VMEM((ROWS, 1), F32), pltpu.VMEM((aw, LANES), F32)]),
        out_shape=[jax.ShapeDtypeStruct((db, 1, aw), F32), jax.ShapeDtypeStruct((db, nh, LANES), jnp.int32)],
        compiler_params=_params(("arbitrary", "arbitrary")),
        name="decode_scan",
    )(pt, q, kn, vn, lfn, qb, *([kc] * pg), *([vc] * pg), *([lfc] * pg), *([mkc] * pg))


def _moba_gate_step(j, q_ref, k_refs, sel_ref, km_ref, *, pg, nh, page, n_blocks):
    per_block = MOBA_BLOCK // page

    @pl.when(j == 0)
    def _():
        km_ref[...] = jnp.zeros(km_ref.shape, F32)

    km = km_ref[...]
    lane = lax.broadcasted_iota(jnp.int32, km.shape, 1)
    for g in range(pg // per_block):
        tot = k_refs[g * per_block][0]
        for u in range(1, per_block):
            tot = tot + k_refs[g * per_block + u][0]
        mean = jnp.sum(tot, axis=1, keepdims=True) * (1.0 / MOBA_BLOCK)
        km = jnp.where(lane == j * (pg // per_block) + g, mean, km)
    km_ref[...] = km

    @pl.when(j == pl.num_programs(1) - 1)
    def _():
        qrows, _ = _head_rows(q_ref[0], nh)
        gate = _nn(qrows, km_ref[...], precision=HIGHEST)
        lane = lax.broadcasted_iota(jnp.int32, gate.shape, 1)
        g = jnp.where(lane < n_blocks, gate, NEG_INF)
        lane_f = lane.astype(F32)
        out = jnp.zeros(gate.shape, F32)
        for t in range(MOBA_TOPK):
            m = jnp.max(g, axis=1, keepdims=True)
            idx = jnp.min(jnp.where(g == m, lane_f, 1e9), axis=1, keepdims=True)
            out = jnp.where(lane == t, idx, out)
            g = jnp.where(lane_f == idx, -3e38, g)
        sel_ref[0] = out[:nh].astype(jnp.int32)


def _moba_decode_kernel(pt_ref, sel_ref, q_ref, kn_ref, vn_ref, *refs, n_pages, hs):
    k_refs, v_refs, o_ref = refs[:hs * n_pages], refs[hs * n_pages:2 * hs * n_pages], refs[2 * hs * n_pages]
    outs = []
    for h in range(hs):
        q, kn, vn = q_ref[0, 0, h:h + 1, :], kn_ref[0, 0, h:h + 1, :], vn_ref[0, 0, h:h + 1, :]
        q16 = jnp.broadcast_to(q, (ROWS, HEAD_DIM)).astype(BF16)
        s_new = jnp.sum(q * kn, axis=1, keepdims=True)
        ss = [_nn(q16, k_refs[h * n_pages + r][0].astype(BF16)) for r in range(n_pages)]
        m = s_new
        for s in ss:
            m = jnp.maximum(m, jnp.max(s, axis=1, keepdims=True))
        p_new = jnp.exp(s_new - m)
        l = p_new
        acc = p_new * vn
        for r, s in enumerate(ss):
            p = jnp.exp(s - m)
            l = l + jnp.sum(p, axis=1, keepdims=True)
            acc = acc + _nt(p.astype(BF16), v_refs[h * n_pages + r][0].astype(BF16))
        outs.append((acc / l)[0:1])
    o_ref[0, 0] = jnp.concatenate(outs, axis=0)


def _moba_decode(pt, sel, q, kn, vn, kc, vc, *, hs=2):
    db, ng, _, hd = q.shape
    page = kc.shape[2]
    per_block = MOBA_BLOCK // page
    n_pages = MOBA_TOPK * per_block
    vec = pl.BlockSpec((1, 1, hs, hd), lambda b, g, pt, sel: (b, g, 0, 0))

    def pspec(e, r):
        t, u = divmod(r, per_block)
        return pl.BlockSpec(
            (1, hd, page),
            lambda b, g, pt, sel: (pt[b, sel[b, (g * hs + e) * MOBA_TOPK + t] * per_block + u], g * hs + e, 0))

    pages = [pspec(e, r) for e in range(hs) for r in range(n_pages)]
    return pl.pallas_call(
        functools.partial(_moba_decode_kernel, n_pages=n_pages, hs=hs),
        grid_spec=pltpu.PrefetchScalarGridSpec(
            num_scalar_prefetch=2, grid=(db, ng), in_specs=[vec, vec, vec] + pages + pages, out_specs=vec),
        out_shape=jax.ShapeDtypeStruct((db, ng, hs, hd), F32),
        compiler_params=_params(("arbitrary", "arbitrary")),
        name="moba_decode",
    )(pt, sel, q, kn, vn, *([kc] * (hs * n_pages)), *([vc] * (hs * n_pages)))


def _rope_tables(pos):
    inv = ROPE_THETA ** (-jnp.arange(HALF, dtype=F32) / HALF)
    ang = pos.astype(F32)[:, None] * inv[None, :]
    cos, sin, zero = jnp.cos(ang), jnp.sin(ang), jnp.zeros_like(ang)
    reps = LANES // HEAD_DIM
    return (jnp.tile(cos, (1, 2 * reps)), jnp.tile(jnp.concatenate([-sin, zero], axis=1), (1, reps)),
            jnp.tile(jnp.concatenate([zero, sin], axis=1), (1, reps)))


def _pick(n, candidates):
    for c in candidates:
        if n % c == 0:
            return c
    raise ValueError(f"no tile for {n}")


def kernel(x_prompt, x_sample, cache_fox_k, cache_fox_v, cache_fox_logf, cache_moba_k, cache_moba_v, state_ffn_conv, page_table, norm_mix_g, w_in, b_forget, w_fox_o, w_moba_o, w_out, norm_ffn_g, w_up, conv_w, conv_b, w_down, norm_final_g):
    batch, seq, d = x_prompt.shape
    db, dseq, _ = x_sample.shape
    depth, n_pool, page, nh, hd = cache_fox_k.shape
    aw = nh * hd
    dff = w_down.shape[1]
    n_pages = page_table.shape[1]
    past = n_pages * page
    assert depth == 1 and hd == HEAD_DIM and cache_moba_k.shape == cache_fox_k.shape and dseq == 1
    assert seq // MOBA_BLOCK <= HEAD_DIM and past // MOBA_BLOCK <= LANES
    assert seq % MOBA_BLOCK == 0 and past % MOBA_BLOCK == 0 and past // MOBA_BLOCK >= MOBA_TOPK
    assert MOBA_BLOCK % page == 0 and aw % LANES == 0 and nh <= 8

    n = batch * seq
    tm = _pick(seq, (512, 256, 128))
    t_attn = _pick(seq, (1024, 512, MOBA_BLOCK))
    tf = _pick(dff, (1408, 1024, 512, 256, 128))
    pg = _pick(n_pages, (16, 8, 4, 2))
    tables_p = _rope_tables(jnp.arange(seq))
    tables_s = _rope_tables(jnp.full((db,), past))

    xp = x_prompt.reshape(n, d)
    xs = x_sample.reshape(db, d)
    feat_major = lambda c: jnp.transpose(c, (0, 1, 3, 4, 2)).reshape(depth * n_pool, aw, page)
    fk, fv, mk, mv = (feat_major(c) for c in (cache_fox_k, cache_fox_v, cache_moba_k, cache_moba_v))
    fl = jnp.transpose(cache_fox_logf, (0, 1, 3, 2)).reshape(depth * n_pool, nh, page)
    pos_major = lambda t: jnp.transpose(t.reshape(batch, nh, -1, seq), (0, 3, 1, 2))
    sizes = (aw, aw, aw, nh, aw, aw, aw, d, d)
    offs = [sum(sizes[:i]) for i in range(len(sizes) + 1)]

    p_new, s_new = [], []
    for l in range(depth):
        cols = [w_in[l][:, offs[i]:offs[i + 1]] for i in range(len(sizes))]
        qa, ka, va, fa, qb, kb, vb, ga, gb = cols
        w_cat = jnp.concatenate([qa * ATTN_SCALE, ka, va, qb * ATTN_SCALE, kb, vb, ga, gb,
                                 jnp.pad(fa, ((0, 0), (0, LANES - nh)))], axis=1).astype(BF16)
        bf = jnp.pad(b_forget[l].astype(F32), (0, LANES - nh)).reshape(1, LANES)
        g_mix = norm_mix_g[l].reshape(1, d)
        g_ffn = norm_ffn_g[l].reshape(1, d)
        wfo, wmo, wout = w_fox_o[l].astype(BF16), w_moba_o[l].astype(BF16), w_out[l].astype(BF16)
        wup, wd = w_up[l].astype(BF16), w_down[l].astype(BF16)
        cw, cb = conv_w[l], conv_b[l].reshape(1, 2 * dff)
        gf = norm_final_g.reshape(1, d)
        pt = page_table + l * n_pool

        qa_p, ka_p, va_p, qb_p, kb_p, vb_p, ga_p, gb_p, lf_p = _inproj(
            xp, g_mix, w_cat, bf, *tables_p, tm=tm, aw=aw, nh=nh, seq=seq)
        oa_p = _fox_prompt(qa_p, ka_p, va_p, lf_p, batch=batch, seq=seq, t=t_attn)
        ob_p = _moba_prompt(qb_p, kb_p, vb_p, batch=batch, seq=seq, t=t_attn)
        x1_p, h2_p = _merge(oa_p, ob_p, ga_p, gb_p, xp, wfo, wmo, wout, g_ffn, tm=tm)
        xp, tails = _ffn_prompt(h2_p, x1_p, wup, cw, cb, wd, gf, seq=seq, tm=tm, sw=_pick(dff, (256, LANES)))
        tps = seq // tm
        tail_p = tails[tps - 1::tps, 8 - (CONV_W - 1):]
        p_new.append((pos_major(ka_p), pos_major(va_p), jnp.transpose(lf_p, (0, 2, 1)), pos_major(kb_p),
                      pos_major(vb_p), tail_p))

        qa_s, ka_s, va_s, qb_s, kb_s, vb_s, ga_s, gb_s, lf_s = _inproj(
            xs, g_mix, w_cat, bf, *tables_s, tm=db, aw=aw, nh=nh)
        r1 = lambda t: t.reshape(db, 1, aw)
        r3 = lambda t: t.reshape(db, nh // 2, 2, hd)
        oa_s, sel = _decode_scan(pt, r1(qa_s), r1(ka_s), r1(va_s), lf_s.reshape(db, nh, 1), r1(qb_s),
                                 fk, fv, fl, mk, pg=pg)
        sel = sel[:, :, :MOBA_TOPK].reshape(db, nh * MOBA_TOPK)
        ob_s = _moba_decode(pt, sel, r3(qb_s), r3(kb_s), r3(vb_s), mk, mv)
        x1_s, h2_s = _merge(oa_s.reshape(db, aw), ob_s.reshape(db, aw), ga_s, gb_s, xs, wfo, wmo, wout, g_ffn, tm=db)
        prev = state_ffn_conv[l]
        xs, ua, ug = _ffn_sample(h2_s, x1_s, prev.reshape(db, (CONV_W - 1) * 2 * dff), wup, cw, cb, wd, gf, tf=tf)
        tail_s = jnp.concatenate([prev[:, 1:], jnp.concatenate([ua, ug], axis=-1)[:, None]], axis=1)
        s_new.append((ka_s.reshape(db, 1, nh, hd), va_s.reshape(db, 1, nh, hd), lf_s.reshape(db, 1, nh),
                      kb_s.reshape(db, 1, nh, hd), vb_s.reshape(db, 1, nh, hd), tail_s))

    p_out = [jnp.stack(z) for z in zip(*p_new)]
    s_out = [jnp.stack(z) for z in zip(*s_new)]
    return (xp.reshape(batch, seq, d), xs.reshape(db, 1, d), *p_out, *s_out)
```

```python
import functools

import jax
import jax.numpy as jnp
from jax import lax
from jax.experimental import pallas as pl
from jax.experimental.pallas import tpu as pltpu

HEAD_DIM = 64
HALF = HEAD_DIM // 2
LANES = 128
MOBA_BLOCK = 256
MOBA_TOPK = 3
CONV_W = 3
ROPE_THETA = 10000.0
RMS_EPS = 1e-6
NEG_INF = -1e30
MASK_BIAS = -2e30
ATTN_SCALE = HEAD_DIM ** -0.5
LOG2E = 1.4426950408889634
VMEM_LIMIT = 56 * 1024 * 1024

F32 = jnp.float32
BF16 = jnp.bfloat16
HIGHEST = lax.Precision.HIGHEST


def _nt(a, b, precision=None):
    return lax.dot_general(a, b, (((1,), (1,)), ((), ())), preferred_element_type=F32, precision=precision)


def _nn(a, b, precision=None):
    return jnp.dot(a, b, preferred_element_type=F32, precision=precision)


def _params(sem):
    return pltpu.CompilerParams(dimension_semantics=sem, vmem_limit_bytes=VMEM_LIMIT)


def _resident(shape):
    nd = len(shape)
    return pl.BlockSpec(shape, lambda *_: (0,) * nd, pipeline_mode=pl.Buffered(1))


def _inproj_kernel(x_ref, g_ref, w_ref, bf_ref, cos_ref, sina_ref, sinb_ref,
                   qa_ref, ka_ref, va_ref, qb_ref, kb_ref, vb_ref, ga_ref, gb_ref, lf_ref,
                   *, aw, d_model, nh, position_minor):
    x = x_ref[...]
    h = (x * lax.rsqrt(jnp.mean(x * x, axis=-1, keepdims=True) + RMS_EPS) * g_ref[...]).astype(BF16)

    def proj(c0, width):
        return _nn(h, w_ref[:, c0:c0 + width])

    def put(o_ref, val):
        if position_minor:
            o_ref[0] = val.T
        else:
            o_ref[...] = val

    cos, sina, sinb = cos_ref[...], sina_ref[...], sinb_ref[...]

    def rope(t):
        out = []
        for s in range(aw // LANES):
            xs = t[:, s * LANES:(s + 1) * LANES]
            out.append(xs * cos + pltpu.roll(xs, LANES - HALF, 1) * sina + pltpu.roll(xs, HALF, 1) * sinb)
        return jnp.concatenate(out, axis=1)

    qa_ref[...] = proj(0, aw)
    put(ka_ref, proj(aw, aw))
    put(va_ref, proj(2 * aw, aw))
    qb_ref[...] = rope(proj(3 * aw, aw))
    put(kb_ref, rope(proj(4 * aw, aw)))
    put(vb_ref, proj(5 * aw, aw))
    ga_ref[...] = jax.nn.sigmoid(proj(6 * aw, d_model)).astype(ga_ref.dtype)
    gb_ref[...] = jax.nn.sigmoid(proj(6 * aw + d_model, d_model)).astype(gb_ref.dtype)

    f = proj(6 * aw + 2 * d_model, LANES) + bf_ref[...]
    lsig = -(jnp.maximum(-f, 0.0) + jnp.log1p(jnp.exp(-jnp.abs(f))))
    if position_minor:
        lf_ref[0] = lsig.T[:nh, :]
    else:
        lf_ref[...] = lsig[:, :nh]


def _inproj(x, g, w, bf, cos, sina, sinb, *, tm, aw, nh, seq=None):
    n, d = x.shape
    nt = n // tm
    tb = cos.shape[0] // tm
    row = lambda i: (i, 0)
    tbl = lambda i: (i % tb, 0)
    nat = (jax.ShapeDtypeStruct((n, aw), F32), pl.BlockSpec((tm, aw), row))
    gate = (jax.ShapeDtypeStruct((n, d), BF16), pl.BlockSpec((tm, d), row))
    if seq is None:
        kv = nat
        lf = (jax.ShapeDtypeStruct((n, nh), F32), pl.BlockSpec((tm, nh), row))
    else:
        tps = seq // tm
        pm = lambda i: (i // tps, 0, i % tps)
        kv = (jax.ShapeDtypeStruct((n // seq, aw, seq), F32), pl.BlockSpec((1, aw, tm), pm))
        lf = (jax.ShapeDtypeStruct((n // seq, nh, seq), F32), pl.BlockSpec((1, nh, tm), pm))
    outs = [nat, kv, kv, nat, kv, kv, gate, gate, lf]
    return pl.pallas_call(
        functools.partial(_inproj_kernel, aw=aw, d_model=d, nh=nh, position_minor=seq is not None),
        grid=(nt,),
        in_specs=[pl.BlockSpec((tm, d), row), _resident(g.shape), _resident(w.shape), _resident(bf.shape),
                  pl.BlockSpec((tm, LANES), tbl), pl.BlockSpec((tm, LANES), tbl), pl.BlockSpec((tm, LANES), tbl)],
        out_specs=[o[1] for o in outs],
        out_shape=[o[0] for o in outs],
        compiler_params=_params(("arbitrary",)),
        name="inproj",
    )(x, g, w, bf, cos, sina, sinb)


def _attn_init(m_ref, l_ref, acc_ref):
    m_ref[...] = jnp.full(m_ref.shape, NEG_INF, F32)
    l_ref[...] = jnp.zeros(l_ref.shape, F32)
    acc_ref[...] = jnp.zeros(acc_ref.shape, F32)


def _flash_causal(i, qts, k_refs, vt_ref, o_ref, *, t):
    def block(j, carry, masked=False):
        off = pl.multiple_of(j * t, t)
        out = []
        for e in range(2):
            m_old, l_old, acc = carry[e]
            s = _nn(k_refs[e][pl.ds(off, t), :], qts[e])
            if masked:
                kpos = lax.broadcasted_iota(jnp.int32, (t, t), 0)
                qpos = lax.broadcasted_iota(jnp.int32, (t, t), 1)
                s = jnp.where(kpos <= qpos, s, NEG_INF)
            m_new = jnp.maximum(m_old, jnp.max(s, axis=0, keepdims=True))
            alpha = jnp.exp2(m_old - m_new)
            p = jnp.exp2(s - m_new)
            l_new = alpha * l_old + jnp.sum(p, axis=0, keepdims=True)
            vt = vt_ref[e * HEAD_DIM:(e + 1) * HEAD_DIM, pl.ds(off, t)]
            out.append((m_new, l_new, alpha * acc + _nn(vt, p.astype(BF16))))
        return tuple(out)

    init = tuple((jnp.full((1, t), NEG_INF, F32), jnp.zeros((1, t), F32), jnp.zeros((HEAD_DIM, t), F32))
                 for _ in range(2))
    carry = lax.fori_loop(0, i // 2, lambda jj, c: block(2 * jj + 1, block(2 * jj, c)), init)
    carry = lax.cond(i % 2 == 1, lambda c: block(i - 1, c), lambda c: c, carry)
    carry = block(i, carry, masked=True)
    o_ref[...] = jnp.concatenate([acc / l for _, l, acc in carry], axis=0).T.astype(o_ref.dtype)


def _split3(x):
    hi = x.astype(BF16).astype(F32)
    mid = (x - hi).astype(BF16).astype(F32)
    lo = (x - hi - mid).astype(BF16).astype(F32)
    return hi, mid, lo


def _fox_prompt_kernel(q_ref, k_ref, v_ref, lft_ref, o_ref, ka16, kb16, vt16, *, t, seq):
    i = pl.program_id(2)

    @pl.when(i == 0)
    def _():
        c = lft_ref[0, 0]
        pos = lax.broadcasted_iota(jnp.int32, c.shape, 1)
        sh = 1
        while sh < seq:
            c = c + jnp.where(pos >= sh, pltpu.roll(c, sh, 1), 0.0)
            sh *= 2
        sub = lax.broadcasted_iota(jnp.int32, (8, seq), 0)
        rows = []
        for e in range(2):
            parts = _split3(-LOG2E * c[e:e + 1])
            rows.append(sum(jnp.where(sub == r, jnp.broadcast_to(part, (8, seq)), 0.0)
                            for r, part in enumerate(parts)))
        fill = jnp.zeros((HEAD_DIM - 8, seq), F32)
        decay = jnp.concatenate([rows[1], fill, rows[0], fill], axis=0).T
        k = k_ref[0].T
        lane = lax.broadcasted_iota(jnp.int32, k.shape, 1)
        ka16[...] = jnp.where(lane < HEAD_DIM, k, decay).astype(BF16)
        kb16[...] = jnp.where(lane >= HEAD_DIM, k, decay).astype(BF16)
        vt16[...] = v_ref[0].astype(BF16)

    qt = (q_ref[...] * LOG2E).T
    ones = jnp.where(lax.broadcasted_iota(jnp.int32, (8, t), 0) < 3, 1.0, 0.0)
    fill = jnp.zeros((HEAD_DIM - 8, t), F32)
    qts = (jnp.concatenate([qt[:HEAD_DIM], ones, fill], axis=0).astype(BF16),
           jnp.concatenate([ones, fill, qt[HEAD_DIM:]], axis=0).astype(BF16))
    _flash_causal(i, qts, (ka16, kb16), vt16, o_ref, t=t)


def _fox_prompt(q, k, v, lft, *, batch, seq, t):
    n, aw = q.shape
    hp = aw // LANES
    nq = seq // t
    lft4 = lft.reshape(batch, hp, 2, seq)
    qmap = lambda b, h, i: (b * nq + i, h)
    kmap = lambda b, h, i: (b, h, 0)
    return pl.pallas_call(
        functools.partial(_fox_prompt_kernel, t=t, seq=seq),
        grid=(batch, hp, nq),
        in_specs=[pl.BlockSpec((t, LANES), qmap), pl.BlockSpec((1, LANES, seq), kmap), pl.BlockSpec((1, LANES, seq), kmap),
                  pl.BlockSpec((1, 1, 2, seq), lambda b, h, i: (b, h, 0, 0))],
        out_specs=pl.BlockSpec((t, LANES), qmap),
        out_shape=jax.ShapeDtypeStruct((n, aw), BF16),
        scratch_shapes=[pltpu.VMEM((seq, LANES), BF16), pltpu.VMEM((seq, LANES), BF16), pltpu.VMEM((LANES, seq), BF16)],
        compiler_params=_params(("arbitrary", "arbitrary", "arbitrary")),
        name="fox_prompt",
    )(q, k, v, lft4)


def _block_bias(gate, limit):
    n = lax.broadcasted_iota(jnp.int32, gate.shape, 0)
    nf = n.astype(F32)
    g = jnp.where(n < limit, gate, NEG_INF)
    sel = jnp.zeros(gate.shape, F32)
    for _ in range(MOBA_TOPK):
        m = jnp.max(g, axis=0, keepdims=True)
        idx = jnp.min(jnp.where(g == m, nf, 1e9), axis=0, keepdims=True)
        hit = nf == idx
        sel = jnp.where(hit & (idx < limit), 1.0, sel)
        g = jnp.where(hit, -3e38, g)
    return jnp.where((n < limit) & (sel < 0.5), MASK_BIAS, 0.0)


def _moba_prompt_kernel(q_ref, k_ref, v_ref, o_ref, ka16, kb16, vt16, km_ref, *, seq, nbp, t):
    bs = MOBA_BLOCK
    nb = seq // bs
    i = pl.program_id(2)

    @pl.when(i == 0)
    def _():
        k = k_ref[0].T
        lane = lax.broadcasted_iota(jnp.int32, k.shape, 1)
        blk = lax.broadcasted_iota(jnp.int32, k.shape, 0) // bs
        ka16[...] = jnp.where(lane < HEAD_DIM, k, jnp.where(lane - HEAD_DIM == blk, 1.0, 0.0)).astype(BF16)
        kb16[...] = jnp.where(lane >= HEAD_DIM, k, jnp.where(lane == blk, 1.0, 0.0)).astype(BF16)
        vt16[...] = v_ref[0].astype(BF16)
        km_ref[...] = jnp.zeros(km_ref.shape, F32)
        for n in range(nb):
            km_ref[n:n + 1, :] = jnp.sum(k[n * bs:(n + 1) * bs, :], axis=0, keepdims=True) * (1.0 / bs)

    qt = q_ref[...].T
    km = km_ref[...]
    lane = lax.broadcasted_iota(jnp.int32, km.shape, 1)
    fill = jnp.zeros((HEAD_DIM - nbp, t), F32)
    own = (i * t + lax.broadcasted_iota(jnp.int32, (1, t), 1)) // bs
    bias_a = _block_bias(_nn(jnp.where(lane < HEAD_DIM, km, 0.0), qt, precision=HIGHEST), own)
    bias_b = _block_bias(_nn(jnp.where(lane >= HEAD_DIM, km, 0.0), qt, precision=HIGHEST), own)
    qs = qt * LOG2E
    qts = (jnp.concatenate([qs[:HEAD_DIM], bias_a, fill], axis=0).astype(BF16),
           jnp.concatenate([bias_b, fill, qs[HEAD_DIM:]], axis=0).astype(BF16))
    _flash_causal(i, qts, (ka16, kb16), vt16, o_ref, t=t)


def _moba_prompt(q, k, v, *, batch, seq, t):
    n, aw = q.shape
    hp = aw // LANES
    nq = seq // t
    nbp = -(-(seq // MOBA_BLOCK) // 8) * 8
    qmap = lambda b, h, i: (b * nq + i, h)
    kmap = lambda b, h, i: (b, h, 0)
    return pl.pallas_call(
        functools.partial(_moba_prompt_kernel, seq=seq, nbp=nbp, t=t),
        grid=(batch, hp, nq),
        in_specs=[pl.BlockSpec((t, LANES), qmap), pl.BlockSpec((1, LANES, seq), kmap), pl.BlockSpec((1, LANES, seq), kmap)],
        out_specs=pl.BlockSpec((t, LANES), qmap),
        out_shape=jax.ShapeDtypeStruct((n, aw), BF16),
        scratch_shapes=[pltpu.VMEM((seq, LANES), BF16), pltpu.VMEM((seq, LANES), BF16), pltpu.VMEM((LANES, seq), BF16),
                        pltpu.VMEM((nbp, LANES), F32)],
        compiler_params=_params(("arbitrary", "arbitrary", "arbitrary")),
        name="moba_prompt",
    )(q, k, v)


def _merge_kernel(oa_ref, ob_ref, ga_ref, gb_ref, x_ref, wfo_ref, wmo_ref, wout_ref, g_ref, x1_ref, h_ref):
    pa = _nn(oa_ref[...].astype(BF16), wfo_ref[...])
    pb = _nn(ob_ref[...].astype(BF16), wmo_ref[...])
    mix = ga_ref[...] * pa + gb_ref[...] * pb
    x1 = x_ref[...] + _nn(mix.astype(BF16), wout_ref[...])
    x1_ref[...] = x1
    h_ref[...] = (x1 * lax.rsqrt(jnp.mean(x1 * x1, axis=-1, keepdims=True) + RMS_EPS) * g_ref[...]).astype(BF16)


def _merge(oa, ob, ga, gb, x, wfo, wmo, wout, g, *, tm):
    n, d = x.shape
    aw = oa.shape[1]
    row = lambda i: (i, 0)
    return pl.pallas_call(
        _merge_kernel,
        grid=(n // tm,),
        in_specs=[pl.BlockSpec((tm, aw), row), pl.BlockSpec((tm, aw), row), pl.BlockSpec((tm, d), row),
                  pl.BlockSpec((tm, d), row), pl.BlockSpec((tm, d), row),
                  _resident(wfo.shape), _resident(wmo.shape), _resident(wout.shape), _resident(g.shape)],
        out_specs=[pl.BlockSpec((tm, d), row), pl.BlockSpec((tm, d), row)],
        out_shape=[jax.ShapeDtypeStruct((n, d), F32), jax.ShapeDtypeStruct((n, d), BF16)],
        compiler_params=_params(("arbitrary",)),
        name="merge",
    )(oa, ob, ga, gb, x, wfo, wmo, wout, g)


def _gated_act(ca, cg):
    gelu = 0.5 * ca * (1.0 + lax.erf(ca * (2.0 ** -0.5)))
    return (gelu * cg).astype(BF16)


def _ffn_prompt_kernel(h_ref, x1_ref, wup_ref, cw_ref, cb_ref, wd_ref, gf_ref, y_ref, tail_ref, prev_ref,
                       *, tiles_per_seq, dff, sw):
    i = pl.program_id(0)
    h = h_ref[...]
    tm = h.shape[0]
    row = lax.broadcasted_iota(jnp.int32, (8, sw), 0)

    @pl.when(i % tiles_per_seq == 0)
    def _():
        prev_ref[...] = jnp.zeros(prev_ref.shape, F32)

    def conv(c0):
        u = _nn(h, wup_ref[:, c0:c0 + sw])
        last = u[tm - 8:, :]
        tail_ref[0, :, c0:c0 + sw] = last
        prev = prev_ref[:, c0:c0 + sw]
        prev_ref[:, c0:c0 + sw] = last
        s1 = pltpu.roll(u, 1, 0)
        s2 = pltpu.roll(u, 2, 0)
        s1 = jnp.concatenate([jnp.where(row < 1, pltpu.roll(prev, 1, 0), s1[:8]), s1[8:]], axis=0)
        s2 = jnp.concatenate([jnp.where(row < 2, pltpu.roll(prev, 2, 0), s2[:8]), s2[8:]], axis=0)
        cw = cw_ref[:, c0:c0 + sw]
        return cb_ref[:, c0:c0 + sw] + cw[0:1] * s2 + cw[1:2] * s1 + cw[2:3] * u

    acts = [_gated_act(conv(c0), conv(dff + c0)) for c0 in range(0, dff, sw)]
    xo = x1_ref[...] + _nn(jnp.concatenate(acts, axis=1), wd_ref[...])
    y_ref[...] = xo * lax.rsqrt(jnp.mean(xo * xo, axis=-1, keepdims=True) + RMS_EPS) * gf_ref[...]


def _ffn_prompt(h, x1, wup, cw, cb, wd, gf, *, seq, tm, sw):
    n, d = x1.shape
    dff = wd.shape[0]
    row = lambda i: (i, 0)
    return pl.pallas_call(
        functools.partial(_ffn_prompt_kernel, tiles_per_seq=seq // tm, dff=dff, sw=sw),
        grid=(n // tm,),
        in_specs=[pl.BlockSpec((tm, d), row), pl.BlockSpec((tm, d), row), _resident(wup.shape), _resident(cw.shape),
                  _resident(cb.shape), _resident(wd.shape), _resident(gf.shape)],
        out_specs=[pl.BlockSpec((tm, d), row), pl.BlockSpec((1, 8, 2 * dff), lambda i: (i, 0, 0))],
        out_shape=[jax.ShapeDtypeStruct((n, d), F32), jax.ShapeDtypeStruct((n // tm, 8, 2 * dff), F32)],
        scratch_shapes=[pltpu.VMEM((8, 2 * dff), F32)],
        compiler_params=_params(("arbitrary",)),
        name="ffn_prompt",
    )(h, x1, wup, cw, cb, wd, gf)


def _ffn_sample_kernel(h_ref, x1_ref, p0a_ref, p0g_ref, p1a_ref, p1g_ref, wa_ref, wg_ref, cwa_ref, cwg_ref,
                       cba_ref, cbg_ref, wd_ref, gf_ref, y_ref, ua_ref, ug_ref, acc_ref, *, n_chunks):
    c = pl.program_id(0)
    h = h_ref[...]

    def conv(w_ref, cw_ref, cb_ref, p0_ref, p1_ref, u_ref):
        u = _nn(h, w_ref[...])
        u_ref[...] = u
        cw = cw_ref[...]
        return cb_ref[...] + cw[0:1] * p0_ref[...] + cw[1:2] * p1_ref[...] + cw[2:3] * u

    ca = conv(wa_ref, cwa_ref, cba_ref, p0a_ref, p1a_ref, ua_ref)
    cg = conv(wg_ref, cwg_ref, cbg_ref, p0g_ref, p1g_ref, ug_ref)
    part = _nn(_gated_act(ca, cg), wd_ref[...])

    @pl.when(c == 0)
    def _():
        acc_ref[...] = part

    @pl.when(c > 0)
    def _():
        acc_ref[...] += part

    @pl.when(c == n_chunks - 1)
    def _():
        xo = x1_ref[...] + acc_ref[...]
        y_ref[...] = xo * lax.rsqrt(jnp.mean(xo * xo, axis=-1, keepdims=True) + RMS_EPS) * gf_ref[...]


def _ffn_sample(h, x1, prev, wup, cw, cb, wd, gf, *, tf):
    n, d = x1.shape
    dff = wd.shape[0]
    nc = dff // tf
    full = lambda c: (0, 0)
    col = lambda k: (lambda c: (0, k * nc + c))
    return pl.pallas_call(
        functools.partial(_ffn_sample_kernel, n_chunks=nc),
        grid=(nc,),
        in_specs=[pl.BlockSpec((n, d), full), pl.BlockSpec((n, d), full),
                  pl.BlockSpec((n, tf), col(0)), pl.BlockSpec((n, tf), col(1)),
                  pl.BlockSpec((n, tf), col(2)), pl.BlockSpec((n, tf), col(3)),
                  pl.BlockSpec((d, tf), col(0)), pl.BlockSpec((d, tf), col(1)),
                  pl.BlockSpec((CONV_W, tf), col(0)), pl.BlockSpec((CONV_W, tf), col(1)),
                  pl.BlockSpec((1, tf), col(0)), pl.BlockSpec((1, tf), col(1)),
                  pl.BlockSpec((tf, d), lambda c: (c, 0)), pl.BlockSpec((1, d), full)],
        out_specs=[pl.BlockSpec((n, d), full), pl.BlockSpec((n, tf), col(0)), pl.BlockSpec((n, tf), col(0))],
        out_shape=[jax.ShapeDtypeStruct((n, d), F32), jax.ShapeDtypeStruct((n, dff), F32),
                   jax.ShapeDtypeStruct((n, dff), F32)],
        scratch_shapes=[pltpu.VMEM((n, d), F32)],
        compiler_params=_params(("arbitrary",)),
        name="ffn_sample",
    )(h, x1, prev, prev, prev, prev, wup, wup, cw, cw, cb, cb, wd, gf)


ROWS = 16


def _pad_rows(x):
    return jnp.concatenate([x, jnp.zeros((ROWS - x.shape[0],) + x.shape[1:], x.dtype)], axis=0)


def _head_rows(q, nh):
    shape = (ROWS, q.shape[1])
    sub = lax.broadcasted_iota(jnp.int32, shape, 0)
    lane = lax.broadcasted_iota(jnp.int32, shape, 1)
    own = lane // HEAD_DIM == sub
    return jnp.where(own, jnp.broadcast_to(q, shape), 0.0), own


def _decode_scan_kernel(pt_ref, q_ref, kn_ref, vn_ref, lfn_ref, qb_ref, *refs, pg, nh, page, n_blocks):
    k_refs, v_refs, lf_refs, mk_refs = (refs[r * pg:(r + 1) * pg] for r in range(4))
    o_ref, sel_ref, m_ref, l_ref, acc_ref, carry_ref, km_ref = refs[4 * pg:]
    j = pl.program_id(1)

    @pl.when(j == 0)
    def _():
        _attn_init(m_ref, l_ref, acc_ref)
        carry_ref[...] = jnp.zeros(carry_ref.shape, F32)

    _moba_gate_step(j, qb_ref, mk_refs, sel_ref, km_ref, pg=pg, nh=nh, page=page, n_blocks=n_blocks)

    qrows, own = _head_rows(q_ref[0], nh)
    q16 = qrows.astype(BF16)
    tri = (lax.broadcasted_iota(jnp.int32, (page, page), 0) <= lax.broadcasted_iota(jnp.int32, (page, page), 1))
    tri = tri.astype(F32)

    local = _nn(jnp.concatenate([_pad_rows(lf_refs[p][0]) for p in range(pg)], axis=0), tri, precision=HIGHEST)
    carry = carry_ref[...]
    scores = []
    for p in range(pg):
        loc = local[p * ROWS:(p + 1) * ROWS]
        scores.append(_nn(q16, k_refs[p][0].astype(BF16)) - (carry + loc))
        carry = carry + loc[:, page - 1:page]
    carry_ref[...] = carry
    s = jnp.concatenate(scores, axis=1)
    m_old = m_ref[...]
    m_new = jnp.maximum(m_old, jnp.max(s, axis=1, keepdims=True))
    alpha = jnp.exp(m_old - m_new)
    pr = jnp.exp(s - m_new)
    p16 = pr.astype(BF16)
    pv = _nt(p16[:, :page], v_refs[0][0].astype(BF16))
    for p in range(1, pg):
        pv = pv + _nt(p16[:, p * page:(p + 1) * page], v_refs[p][0].astype(BF16))
    l_ref[...] = alpha * l_ref[...] + jnp.sum(pr, axis=1, keepdims=True)
    acc_ref[...] = alpha * acc_ref[...] + pv
    m_ref[...] = m_new

    @pl.when(j == pl.num_programs(1) - 1)
    def _():
        lfn = _pad_rows(lfn_ref[0])
        s = jnp.sum(qrows * kn_ref[0], axis=1, keepdims=True) - (carry_ref[...] + lfn)
        m_old = m_ref[...]
        m_new = jnp.maximum(m_old, s)
        alpha = jnp.exp(m_old - m_new)
        p = jnp.exp(s - m_new)
        l = alpha * l_ref[...] + p
        acc = alpha * acc_ref[...] + p * vn_ref[0]
        o_ref[0] = jnp.sum(jnp.where(own, acc / l, 0.0), axis=0, keepdims=True)


def _decode_scan(pt, q, kn, vn, lfn, qb, kc, vc, lfc, mkc, *, pg):
    db, _, aw = q.shape
    nh = aw // HEAD_DIM
    page = kc.shape[2]
    npg = pt.shape[1]
    vec = pl.BlockSpec((1, 1, aw), lambda b, j, pt: (b, 0, 0))
    pspec = lambda shape, p: pl.BlockSpec(shape, lambda b, j, pt: (pt[b, j * pg + p], 0, 0))
    in_specs = [vec, vec, vec, pl.BlockSpec((1, nh, 1), lambda b, j, pt: (b, 0, 0)), vec]
    in_specs += [pspec((1, aw, page), p) for p in range(pg)] * 2
    in_specs += [pspec((1, nh, page), p) for p in range(pg)]
    in_specs += [pspec((1, aw, page), p) for p in range(pg)]
    return pl.pallas_call(
        functools.partial(_decode_scan_kernel, pg=pg, nh=nh, page=page, n_blocks=npg * page // MOBA_BLOCK),
        grid_spec=pltpu.PrefetchScalarGridSpec(
            num_scalar_prefetch=1, grid=(db, npg // pg), in_specs=in_specs,
            out_specs=[vec, pl.BlockSpec((1, nh, LANES), lambda b, j, pt: (b, 0, 0))],
            scratch_shapes=[pltpu.VMEM((ROWS, 1), F32), pltpu.VMEM((ROWS, 1), F32), pltpu.VMEM((ROWS, aw), F32),
                            pltpu.VMEM((ROWS, 1), F32), pltpu.VMEM((aw, LANES), F32)]),
        out_shape=[jax.ShapeDtypeStruct((db, 1, aw), F32), jax.ShapeDtypeStruct((db, nh, LANES), jnp.int32)],
        compiler_params=_params(("arbitrary", "arbitrary")),
        name="decode_scan",
    )(pt, q, kn, vn, lfn, qb, *([kc] * pg), *([vc] * pg), *([lfc] * pg), *([mkc] * pg))


def _moba_gate_step(j, q_ref, k_refs, sel_ref, km_ref, *, pg, nh, page, n_blocks):
    per_block = MOBA_BLOCK // page

    @pl.when(j == 0)
    def _():
        km_ref[...] = jnp.zeros(km_ref.shape, F32)

    km = km_ref[...]
    lane = lax.broadcasted_iota(jnp.int32, km.shape, 1)
    for g in range(pg // per_block):
        tot = k_refs[g * per_block][0]
        for u in range(1, per_block):
            tot = tot + k_refs[g * per_block + u][0]
        mean = jnp.sum(tot, axis=1, keepdims=True) * (1.0 / MOBA_BLOCK)
        km = jnp.where(lane == j * (pg // per_block) + g, mean, km)
    km_ref[...] = km

    @pl.when(j == pl.num_programs(1) - 1)
    def _():
        qrows, _ = _head_rows(q_ref[0], nh)
        gate = _nn(qrows, km_ref[...], precision=HIGHEST)
        lane = lax.broadcasted_iota(jnp.int32, gate.shape, 1)
        g = jnp.where(lane < n_blocks, gate, NEG_INF)
        lane_f = lane.astype(F32)
        out = jnp.zeros(gate.shape, F32)
        for t in range(MOBA_TOPK):
            m = jnp.max(g, axis=1, keepdims=True)
            idx = jnp.min(jnp.where(g == m, lane_f, 1e9), axis=1, keepdims=True)
            out = jnp.where(lane == t, idx, out)
            g = jnp.where(lane_f == idx, -3e38, g)
        sel_ref[0] = out[:nh].astype(jnp.int32)


def _moba_decode_kernel(pt_ref, sel_ref, q_ref, kn_ref, vn_ref, *refs, n_pages, hs):
    k_refs, v_refs, o_ref = refs[:hs * n_pages], refs[hs * n_pages:2 * hs * n_pages], refs[2 * hs * n_pages]
    qrows, own = _head_rows(q_ref[0, 0], hs)
    q16 = qrows.astype(BF16)

    def stacked(page_refs, r):
        return jnp.concatenate([page_refs[e * n_pages + r][0] for e in range(hs)], axis=0).astype(BF16)

    s_new = jnp.sum(qrows * kn_ref[0, 0], axis=1, keepdims=True)
    ss = [_nn(q16, stacked(k_refs, r)) for r in range(n_pages)]
    m = s_new
    for s in ss:
        m = jnp.maximum(m, jnp.max(s, axis=1, keepdims=True))
    p_new = jnp.exp(s_new - m)
    l = p_new
    acc = p_new * vn_ref[0, 0]
    for r, s in enumerate(ss):
        p = jnp.exp(s - m)
        l = l + jnp.sum(p, axis=1, keepdims=True)
        acc = acc + _nt(p.astype(BF16), stacked(v_refs, r))
    o_ref[0, 0] = jnp.sum(jnp.where(own, acc / l, 0.0), axis=0, keepdims=True)


def _moba_decode(pt, sel, q, kn, vn, kc, vc):
    db, ng, _, gw = q.shape
    hd = HEAD_DIM
    hs = gw // hd
    page = kc.shape[2]
    per_block = MOBA_BLOCK // page
    n_pages = MOBA_TOPK * per_block
    vec = pl.BlockSpec((1, 1, 1, gw), lambda b, g, pt, sel: (b, g, 0, 0))

    def pspec(e, r):
        t, u = divmod(r, per_block)
        return pl.BlockSpec(
            (1, hd, page),
            lambda b, g, pt, sel: (pt[b, sel[b, (g * hs + e) * MOBA_TOPK + t] * per_block + u], g * hs + e, 0))

    pages = [pspec(e, r) for e in range(hs) for r in range(n_pages)]
    return pl.pallas_call(
        functools.partial(_moba_decode_kernel, n_pages=n_pages, hs=hs),
        grid_spec=pltpu.PrefetchScalarGridSpec(
            num_scalar_prefetch=2, grid=(db, ng), in_specs=[vec, vec, vec] + pages + pages, out_specs=vec),
        out_shape=jax.ShapeDtypeStruct((db, ng, 1, gw), F32),
        compiler_params=_params(("arbitrary", "arbitrary")),
        name="moba_decode",
    )(pt, sel, q, kn, vn, *([kc] * (hs * n_pages)), *([vc] * (hs * n_pages)))


def _rope_tables(pos):
    inv = ROPE_THETA ** (-jnp.arange(HALF, dtype=F32) / HALF)
    ang = pos.astype(F32)[:, None] * inv[None, :]
    cos, sin, zero = jnp.cos(ang), jnp.sin(ang), jnp.zeros_like(ang)
    reps = LANES // HEAD_DIM
    return (jnp.tile(cos, (1, 2 * reps)), jnp.tile(jnp.concatenate([-sin, zero], axis=1), (1, reps)),
            jnp.tile(jnp.concatenate([zero, sin], axis=1), (1, reps)))


def _pick(n, candidates):
    for c in candidates:
        if n % c == 0:
            return c
    raise ValueError(f"no tile for {n}")


def kernel(x_prompt, x_sample, cache_fox_k, cache_fox_v, cache_fox_logf, cache_moba_k, cache_moba_v, state_ffn_conv, page_table, norm_mix_g, w_in, b_forget, w_fox_o, w_moba_o, w_out, norm_ffn_g, w_up, conv_w, conv_b, w_down, norm_final_g):
    batch, seq, d = x_prompt.shape
    db, dseq, _ = x_sample.shape
    depth, n_pool, page, nh, hd = cache_fox_k.shape
    aw = nh * hd
    dff = w_down.shape[1]
    n_pages = page_table.shape[1]
    past = n_pages * page
    assert depth == 1 and hd == HEAD_DIM and cache_moba_k.shape == cache_fox_k.shape and dseq == 1
    assert seq // MOBA_BLOCK <= HEAD_DIM and past // MOBA_BLOCK <= LANES
    assert seq % MOBA_BLOCK == 0 and past % MOBA_BLOCK == 0 and past // MOBA_BLOCK >= MOBA_TOPK
    assert MOBA_BLOCK % page == 0 and aw % LANES == 0 and nh <= 8

    n = batch * seq
    tm = _pick(seq, (512, 256, 128))
    t_attn = _pick(seq, (1024, 512, MOBA_BLOCK))
    tf = _pick(dff, (1408, 1024, 512, 256, 128))
    pg = _pick(n_pages, (16, 8, 4, 2))
    tables_p = _rope_tables(jnp.arange(seq))
    tables_s = _rope_tables(jnp.full((db,), past))

    xp = x_prompt.reshape(n, d)
    xs = x_sample.reshape(db, d)
    feat_major = lambda c: jnp.transpose(c, (0, 1, 3, 4, 2)).reshape(depth * n_pool, aw, page)
    fk, fv, mk, mv = (feat_major(c) for c in (cache_fox_k, cache_fox_v, cache_moba_k, cache_moba_v))
    fl = jnp.transpose(cache_fox_logf, (0, 1, 3, 2)).reshape(depth * n_pool, nh, page)
    pos_major = lambda t: jnp.transpose(t.reshape(batch, nh, -1, seq), (0, 3, 1, 2))
    sizes = (aw, aw, aw, nh, aw, aw, aw, d, d)
    offs = [sum(sizes[:i]) for i in range(len(sizes) + 1)]

    p_new, s_new = [], []
    for l in range(depth):
        cols = [w_in[l][:, offs[i]:offs[i + 1]] for i in range(len(sizes))]
        qa, ka, va, fa, qb, kb, vb, ga, gb = cols
        w_cat = jnp.concatenate([qa * ATTN_SCALE, ka, va, qb * ATTN_SCALE, kb, vb, ga, gb,
                                 jnp.pad(fa, ((0, 0), (0, LANES - nh)))], axis=1).astype(BF16)
        bf = jnp.pad(b_forget[l].astype(F32), (0, LANES - nh)).reshape(1, LANES)
        g_mix = norm_mix_g[l].reshape(1, d)
        g_ffn = norm_ffn_g[l].reshape(1, d)
        wfo, wmo, wout = w_fox_o[l].astype(BF16), w_moba_o[l].astype(BF16), w_out[l].astype(BF16)
        wup, wd = w_up[l].astype(BF16), w_down[l].astype(BF16)
        cw, cb = conv_w[l], conv_b[l].reshape(1, 2 * dff)
        gf = norm_final_g.reshape(1, d)
        pt = page_table + l * n_pool

        qa_p, ka_p, va_p, qb_p, kb_p, vb_p, ga_p, gb_p, lf_p = _inproj(
            xp, g_mix, w_cat, bf, *tables_p, tm=tm, aw=aw, nh=nh, seq=seq)
        oa_p = _fox_prompt(qa_p, ka_p, va_p, lf_p, batch=batch, seq=seq, t=t_attn)
        ob_p = _moba_prompt(qb_p, kb_p, vb_p, batch=batch, seq=seq, t=t_attn)
        x1_p, h2_p = _merge(oa_p, ob_p, ga_p, gb_p, xp, wfo, wmo, wout, g_ffn, tm=tm)
        xp, tails = _ffn_prompt(h2_p, x1_p, wup, cw, cb, wd, gf, seq=seq, tm=tm, sw=_pick(dff, (256, LANES)))
        tps = seq // tm
        tail_p = tails[tps - 1::tps, 8 - (CONV_W - 1):]
        p_new.append((pos_major(ka_p), pos_major(va_p), jnp.transpose(lf_p, (0, 2, 1)), pos_major(kb_p),
                      pos_major(vb_p), tail_p))

        qa_s, ka_s, va_s, qb_s, kb_s, vb_s, ga_s, gb_s, lf_s = _inproj(
            xs, g_mix, w_cat, bf, *tables_s, tm=db, aw=aw, nh=nh)
        r1 = lambda t: t.reshape(db, 1, aw)
        hs = _pick(nh, (4, 2, 1))
        r3 = lambda t: t.reshape(db, nh // hs, 1, hs * hd)
        oa_s, sel = _decode_scan(pt, r1(qa_s), r1(ka_s), r1(va_s), lf_s.reshape(db, nh, 1), r1(qb_s),
                                 fk, fv, fl, mk, pg=pg)
        sel = sel[:, :, :MOBA_TOPK].reshape(db, nh * MOBA_TOPK)
        ob_s = _moba_decode(pt, sel, r3(qb_s), r3(kb_s), r3(vb_s), mk, mv)
        x1_s, h2_s = _merge(oa_s.reshape(db, aw), ob_s.reshape(db, aw), ga_s, gb_s, xs, wfo, wmo, wout, g_ffn, tm=db)
        prev = state_ffn_conv[l]
        xs, ua, ug = _ffn_sample(h2_s, x1_s, prev.reshape(db, (CONV_W - 1) * 2 * dff), wup, cw, cb, wd, gf, tf=tf)
        tail_s = jnp.concatenate([prev[:, 1:], jnp.concatenate([ua, ug], axis=-1)[:, None]], axis=1)
        s_new.append((ka_s.reshape(db, 1, nh, hd), va_s.reshape(db, 1, nh, hd), lf_s.reshape(db, 1, nh),
                      kb_s.reshape(db, 1, nh, hd), vb_s.reshape(db, 1, nh, hd), tail_s))

    p_out = [jnp.stack(z) for z in zip(*p_new)]
    s_out = [jnp.stack(z) for z in zip(*s_new)]
    return (xp.reshape(batch, seq, d), xs.reshape(db, 1, d), *p_out, *s_out)
```

```python
import functools

import jax
import jax.numpy as jnp
from jax import lax
from jax.experimental import pallas as pl
from jax.experimental.pallas import tpu as pltpu

HEAD_DIM = 64
HALF = HEAD_DIM // 2
LANES = 128
MOBA_BLOCK = 256
MOBA_TOPK = 3
CONV_W = 3
ROPE_THETA = 10000.0
RMS_EPS = 1e-6
NEG_INF = -1e30
MASK_BIAS = -2e30
ATTN_SCALE = HEAD_DIM ** -0.5
LOG2E = 1.4426950408889634
VMEM_LIMIT = 56 * 1024 * 1024

F32 = jnp.float32
BF16 = jnp.bfloat16
HIGHEST = lax.Precision.HIGHEST


def _nt(a, b, precision=None):
    return lax.dot_general(a, b, (((1,), (1,)), ((), ())), preferred_element_type=F32, precision=precision)


def _nn(a, b, precision=None):
    return jnp.dot(a, b, preferred_element_type=F32, precision=precision)


def _params(sem):
    return pltpu.CompilerParams(dimension_semantics=sem, vmem_limit_bytes=VMEM_LIMIT)


def _resident(shape):
    nd = len(shape)
    return pl.BlockSpec(shape, lambda *_: (0,) * nd, pipeline_mode=pl.Buffered(1))


def _inproj_kernel(x_ref, g_ref, w_ref, bf_ref, cos_ref, sina_ref, sinb_ref,
                   qa_ref, ka_ref, va_ref, qb_ref, kb_ref, vb_ref, ga_ref, gb_ref, lf_ref,
                   *, aw, d_model, nh, position_minor):
    x = x_ref[...]
    h = (x * lax.rsqrt(jnp.mean(x * x, axis=-1, keepdims=True) + RMS_EPS) * g_ref[...]).astype(BF16)

    def proj(c0, width):
        return _nn(h, w_ref[:, c0:c0 + width])

    def put(o_ref, val):
        if position_minor:
            o_ref[0] = val.T
        else:
            o_ref[...] = val

    cos, sina, sinb = cos_ref[...], sina_ref[...], sinb_ref[...]

    def rope(t):
        out = []
        for s in range(aw // LANES):
            xs = t[:, s * LANES:(s + 1) * LANES]
            out.append(xs * cos + pltpu.roll(xs, LANES - HALF, 1) * sina + pltpu.roll(xs, HALF, 1) * sinb)
        return jnp.concatenate(out, axis=1)

    qa_ref[...] = proj(0, aw)
    put(ka_ref, proj(aw, aw))
    put(va_ref, proj(2 * aw, aw))
    qb_ref[...] = rope(proj(3 * aw, aw))
    put(kb_ref, rope(proj(4 * aw, aw)))
    put(vb_ref, proj(5 * aw, aw))
    ga_ref[...] = jax.nn.sigmoid(proj(6 * aw, d_model)).astype(ga_ref.dtype)
    gb_ref[...] = jax.nn.sigmoid(proj(6 * aw + d_model, d_model)).astype(gb_ref.dtype)

    f = proj(6 * aw + 2 * d_model, LANES) + bf_ref[...]
    lsig = -(jnp.maximum(-f, 0.0) + jnp.log1p(jnp.exp(-jnp.abs(f))))
    if position_minor:
        lf_ref[0] = lsig.T[:nh, :]
    else:
        lf_ref[...] = lsig[:, :nh]


def _inproj(x, g, w, bf, cos, sina, sinb, *, tm, aw, nh, seq=None):
    n, d = x.shape
    nt = n // tm
    tb = cos.shape[0] // tm
    row = lambda i: (i, 0)
    tbl = lambda i: (i % tb, 0)
    nat = (jax.ShapeDtypeStruct((n, aw), F32), pl.BlockSpec((tm, aw), row))
    gate = (jax.ShapeDtypeStruct((n, d), BF16), pl.BlockSpec((tm, d), row))
    if seq is None:
        kv = nat
        lf = (jax.ShapeDtypeStruct((n, nh), F32), pl.BlockSpec((tm, nh), row))
    else:
        tps = seq // tm
        pm = lambda i: (i // tps, 0, i % tps)
        kv = (jax.ShapeDtypeStruct((n // seq, aw, seq), F32), pl.BlockSpec((1, aw, tm), pm))
        lf = (jax.ShapeDtypeStruct((n // seq, nh, seq), F32), pl.BlockSpec((1, nh, tm), pm))
    outs = [nat, kv, kv, nat, kv, kv, gate, gate, lf]
    return pl.pallas_call(
        functools.partial(_inproj_kernel, aw=aw, d_model=d, nh=nh, position_minor=seq is not None),
        grid=(nt,),
        in_specs=[pl.BlockSpec((tm, d), row), _resident(g.shape), _resident(w.shape), _resident(bf.shape),
                  pl.BlockSpec((tm, LANES), tbl), pl.BlockSpec((tm, LANES), tbl), pl.BlockSpec((tm, LANES), tbl)],
        out_specs=[o[1] for o in outs],
        out_shape=[o[0] for o in outs],
        compiler_params=_params(("arbitrary",)),
        name="inproj",
    )(x, g, w, bf, cos, sina, sinb)


def _attn_init(m_ref, l_ref, acc_ref):
    m_ref[...] = jnp.full(m_ref.shape, NEG_INF, F32)
    l_ref[...] = jnp.zeros(l_ref.shape, F32)
    acc_ref[...] = jnp.zeros(acc_ref.shape, F32)


def _flash_causal(i, qts, k_refs, vt_ref, o_ref, *, t, side_work=None):
    def block(j, carry, masked=False):
        off = pl.multiple_of(j * t, t)
        out = []
        for e in range(2):
            m_old, l_old, acc = carry[e]
            s = _nn(k_refs[e][pl.ds(off, t), :], qts[e])
            if masked:
                kpos = lax.broadcasted_iota(jnp.int32, (t, t), 0)
                qpos = lax.broadcasted_iota(jnp.int32, (t, t), 1)
                s = jnp.where(kpos <= qpos, s, NEG_INF)
            m_new = jnp.maximum(m_old, jnp.max(s, axis=0, keepdims=True))
            alpha = jnp.exp2(m_old - m_new)
            p = jnp.exp2(s - m_new)
            l_new = alpha * l_old + jnp.sum(p, axis=0, keepdims=True)
            vt = vt_ref[e * HEAD_DIM:(e + 1) * HEAD_DIM, pl.ds(off, t)]
            out.append((m_new, l_new, alpha * acc + _nn(vt, p.astype(BF16))))
        return tuple(out)

    init = tuple((jnp.full((1, t), NEG_INF, F32), jnp.zeros((1, t), F32), jnp.zeros((HEAD_DIM, t), F32))
                 for _ in range(2))
    carry = lax.fori_loop(0, i // 2, lambda jj, c: block(2 * jj + 1, block(2 * jj, c)), init)
    carry = lax.cond(i % 2 == 1, lambda c: block(i - 1, c), lambda c: c, carry)
    if side_work is not None:
        side_work()
    carry = block(i, carry, masked=True)
    o_ref[...] = jnp.concatenate([acc / l for _, l, acc in carry], axis=0).T.astype(o_ref.dtype)


def _split3(x):
    hi = x.astype(BF16).astype(F32)
    mid = (x - hi).astype(BF16).astype(F32)
    lo = (x - hi - mid).astype(BF16).astype(F32)
    return hi, mid, lo


def _fox_prompt_kernel(q_ref, k_ref, v_ref, lft_ref, o_ref, ka16, kb16, vt16, *, t, seq):
    i = pl.program_id(2)

    @pl.when(i == 0)
    def _():
        c = lft_ref[0, 0]
        pos = lax.broadcasted_iota(jnp.int32, c.shape, 1)
        sh = 1
        while sh < seq:
            c = c + jnp.where(pos >= sh, pltpu.roll(c, sh, 1), 0.0)
            sh *= 2
        sub = lax.broadcasted_iota(jnp.int32, (8, seq), 0)
        rows = []
        for e in range(2):
            parts = _split3(-LOG2E * c[e:e + 1])
            rows.append(sum(jnp.where(sub == r, jnp.broadcast_to(part, (8, seq)), 0.0)
                            for r, part in enumerate(parts)))
        fill = jnp.zeros((HEAD_DIM - 8, seq), F32)
        decay = jnp.concatenate([rows[1], fill, rows[0], fill], axis=0).T
        k = k_ref[0].T
        lane = lax.broadcasted_iota(jnp.int32, k.shape, 1)
        ka16[...] = jnp.where(lane < HEAD_DIM, k, decay).astype(BF16)
        kb16[...] = jnp.where(lane >= HEAD_DIM, k, decay).astype(BF16)
        vt16[...] = v_ref[0].astype(BF16)

    qt = (q_ref[...] * LOG2E).T
    ones = jnp.where(lax.broadcasted_iota(jnp.int32, (8, t), 0) < 3, 1.0, 0.0)
    fill = jnp.zeros((HEAD_DIM - 8, t), F32)
    qts = (jnp.concatenate([qt[:HEAD_DIM], ones, fill], axis=0).astype(BF16),
           jnp.concatenate([ones, fill, qt[HEAD_DIM:]], axis=0).astype(BF16))
    _flash_causal(i, qts, (ka16, kb16), vt16, o_ref, t=t)


def _fox_prompt(q, k, v, lft, *, batch, seq, t):
    n, aw = q.shape
    hp = aw // LANES
    nq = seq // t
    lft4 = lft.reshape(batch, hp, 2, seq)
    qmap = lambda b, h, i: (b * nq + i, h)
    kmap = lambda b, h, i: (b, h, 0)
    return pl.pallas_call(
        functools.partial(_fox_prompt_kernel, t=t, seq=seq),
        grid=(batch, hp, nq),
        in_specs=[pl.BlockSpec((t, LANES), qmap), pl.BlockSpec((1, LANES, seq), kmap), pl.BlockSpec((1, LANES, seq), kmap),
                  pl.BlockSpec((1, 1, 2, seq), lambda b, h, i: (b, h, 0, 0))],
        out_specs=pl.BlockSpec((t, LANES), qmap),
        out_shape=jax.ShapeDtypeStruct((n, aw), BF16),
        scratch_shapes=[pltpu.VMEM((seq, LANES), BF16), pltpu.VMEM((seq, LANES), BF16), pltpu.VMEM((LANES, seq), BF16)],
        compiler_params=_params(("arbitrary", "arbitrary", "arbitrary")),
        name="fox_prompt",
    )(q, k, v, lft4)


def _block_bias(gate, limit):
    n = lax.broadcasted_iota(jnp.int32, gate.shape, 0)
    nf = n.astype(F32)
    g = jnp.where(n < limit, gate, NEG_INF)
    sel = jnp.zeros(gate.shape, F32)
    for _ in range(MOBA_TOPK):
        m = jnp.max(g, axis=0, keepdims=True)
        idx = jnp.min(jnp.where(g == m, nf, 1e9), axis=0, keepdims=True)
        hit = nf == idx
        sel = jnp.where(hit & (idx < limit), 1.0, sel)
        g = jnp.where(hit, -3e38, g)
    return jnp.where((n < limit) & (sel < 0.5), MASK_BIAS, 0.0)


def _moba_prompt_kernel(pt_ref, q_ref, k_ref, v_ref, qs_ref, *refs, seq, nbp, t, pg, nh, page, n_blocks, sps):
    mk_refs = refs[:pg]
    o_ref, sel_ref, ka16, kb16, vt16, km_ref, kms_ref = refs[pg:]
    bs = MOBA_BLOCK
    nb = seq // bs
    i = pl.program_id(2)
    step = (pl.program_id(0) * pl.num_programs(1) + pl.program_id(1)) * pl.num_programs(2) + i

    @pl.when(step == 0)
    def _():
        kms_ref[...] = jnp.zeros(kms_ref.shape, F32)

    @pl.when(i == 0)
    def _():
        k = k_ref[0].T
        lane = lax.broadcasted_iota(jnp.int32, k.shape, 1)
        blk = lax.broadcasted_iota(jnp.int32, k.shape, 0) // bs
        ka16[...] = jnp.where(lane < HEAD_DIM, k, jnp.where(lane - HEAD_DIM == blk, 1.0, 0.0)).astype(BF16)
        kb16[...] = jnp.where(lane >= HEAD_DIM, k, jnp.where(lane == blk, 1.0, 0.0)).astype(BF16)
        vt16[...] = v_ref[0].astype(BF16)
        km_ref[...] = jnp.zeros(km_ref.shape, F32)
        for n in range(nb):
            km_ref[n:n + 1, :] = jnp.sum(k[n * bs:(n + 1) * bs, :], axis=0, keepdims=True) * (1.0 / bs)

    qt = q_ref[...].T
    km = km_ref[...]
    lane = lax.broadcasted_iota(jnp.int32, km.shape, 1)
    fill = jnp.zeros((HEAD_DIM - nbp, t), F32)
    own = (i * t + lax.broadcasted_iota(jnp.int32, (1, t), 1)) // bs
    bias_a = _block_bias(_nn(jnp.where(lane < HEAD_DIM, km, 0.0), qt, precision=HIGHEST), own)
    bias_b = _block_bias(_nn(jnp.where(lane >= HEAD_DIM, km, 0.0), qt, precision=HIGHEST), own)
    qs = qt * LOG2E
    qts = (jnp.concatenate([qs[:HEAD_DIM], bias_a, fill], axis=0).astype(BF16),
           jnp.concatenate([bias_b, fill, qs[HEAD_DIM:]], axis=0).astype(BF16))
    scan = functools.partial(_moba_gate_accumulate, step % sps, mk_refs, kms_ref, pg=pg, page=page)
    _flash_causal(i, qts, (ka16, kb16), vt16, o_ref, t=t, side_work=scan)
    _moba_gate_select(step % sps, sps, qs_ref, sel_ref, kms_ref, nh=nh, n_blocks=n_blocks)


def _moba_prompt(pt, q, k, v, qs, mkc, *, batch, seq, t):
    n, aw = q.shape
    db, npg = pt.shape
    page = mkc.shape[2]
    hp = aw // LANES
    nh = aw // HEAD_DIM
    nq = seq // t
    nbp = -(-(seq // MOBA_BLOCK) // 8) * 8
    steps = batch * hp * nq
    assert steps % db == 0 and npg % (steps // db) == 0, "sample-row scan does not tile this call's grid"
    sps = steps // db
    pg = npg // sps
    assert pg % (MOBA_BLOCK // page) == 0
    qmap = lambda b, h, i, pt: (b * nq + i, h)
    kmap = lambda b, h, i, pt: (b, h, 0)
    lin = lambda b, h, i: (b * hp + h) * nq + i
    smap = lambda b, h, i, pt: (lin(b, h, i) // sps, 0, 0)
    pmap = lambda p: (lambda b, h, i, pt: (pt[lin(b, h, i) // sps, (lin(b, h, i) % sps) * pg + p], 0, 0))
    in_specs = [pl.BlockSpec((t, LANES), qmap), pl.BlockSpec((1, LANES, seq), kmap), pl.BlockSpec((1, LANES, seq), kmap),
                pl.BlockSpec((1, 1, aw), smap)]
    in_specs += [pl.BlockSpec((1, aw, page), pmap(p)) for p in range(pg)]
    return pl.pallas_call(
        functools.partial(_moba_prompt_kernel, seq=seq, nbp=nbp, t=t, pg=pg, nh=nh, page=page,
                          n_blocks=npg * page // MOBA_BLOCK, sps=sps),
        grid_spec=pltpu.PrefetchScalarGridSpec(
            num_scalar_prefetch=1, grid=(batch, hp, nq), in_specs=in_specs,
            out_specs=[pl.BlockSpec((t, LANES), qmap), pl.BlockSpec((1, nh, LANES), smap)],
            scratch_shapes=[pltpu.VMEM((seq, LANES), BF16), pltpu.VMEM((seq, LANES), BF16),
                            pltpu.VMEM((LANES, seq), BF16), pltpu.VMEM((nbp, LANES), F32),
                            pltpu.VMEM((aw, LANES), F32)]),
        out_shape=[jax.ShapeDtypeStruct((n, aw), BF16),
                   jax.ShapeDtypeStruct((db, nh, LANES), jnp.int32)],
        compiler_params=_params(("arbitrary", "arbitrary", "arbitrary")),
        name="moba_prompt",
    )(pt, q, k, v, qs, *([mkc] * pg))


def _merge_kernel(oa_ref, ob_ref, ga_ref, gb_ref, x_ref, wfo_ref, wmo_ref, wout_ref, g_ref, x1_ref, h_ref):
    pa = _nn(oa_ref[...].astype(BF16), wfo_ref[...])
    pb = _nn(ob_ref[...].astype(BF16), wmo_ref[...])
    mix = ga_ref[...] * pa + gb_ref[...] * pb
    x1 = x_ref[...] + _nn(mix.astype(BF16), wout_ref[...])
    x1_ref[...] = x1
    h_ref[...] = (x1 * lax.rsqrt(jnp.mean(x1 * x1, axis=-1, keepdims=True) + RMS_EPS) * g_ref[...]).astype(BF16)


def _merge(oa, ob, ga, gb, x, wfo, wmo, wout, g, *, tm):
    n, d = x.shape
    aw = oa.shape[1]
    row = lambda i: (i, 0)
    return pl.pallas_call(
        _merge_kernel,
        grid=(n // tm,),
        in_specs=[pl.BlockSpec((tm, aw), row), pl.BlockSpec((tm, aw), row), pl.BlockSpec((tm, d), row),
                  pl.BlockSpec((tm, d), row), pl.BlockSpec((tm, d), row),
                  _resident(wfo.shape), _resident(wmo.shape), _resident(wout.shape), _resident(g.shape)],
        out_specs=[pl.BlockSpec((tm, d), row), pl.BlockSpec((tm, d), row)],
        out_shape=[jax.ShapeDtypeStruct((n, d), F32), jax.ShapeDtypeStruct((n, d), BF16)],
        compiler_params=_params(("arbitrary",)),
        name="merge",
    )(oa, ob, ga, gb, x, wfo, wmo, wout, g)


def _gated_act(ca, cg):
    gelu = 0.5 * ca * (1.0 + lax.erf(ca * (2.0 ** -0.5)))
    return (gelu * cg).astype(BF16)


def _ffn_prompt_kernel(h_ref, x1_ref, wup_ref, cw_ref, cb_ref, wd_ref, gf_ref, y_ref, tail_ref, prev_ref,
                       *, tiles_per_seq, dff, sw):
    i = pl.program_id(0)
    h = h_ref[...]
    tm = h.shape[0]
    row = lax.broadcasted_iota(jnp.int32, (8, sw), 0)

    @pl.when(i % tiles_per_seq == 0)
    def _():
        prev_ref[...] = jnp.zeros(prev_ref.shape, F32)

    def conv(c0):
        u = _nn(h, wup_ref[:, c0:c0 + sw])
        last = u[tm - 8:, :]
        tail_ref[0, :, c0:c0 + sw] = last
        prev = prev_ref[:, c0:c0 + sw]
        prev_ref[:, c0:c0 + sw] = last
        s1 = pltpu.roll(u, 1, 0)
        s2 = pltpu.roll(u, 2, 0)
        s1 = jnp.concatenate([jnp.where(row < 1, pltpu.roll(prev, 1, 0), s1[:8]), s1[8:]], axis=0)
        s2 = jnp.concatenate([jnp.where(row < 2, pltpu.roll(prev, 2, 0), s2[:8]), s2[8:]], axis=0)
        cw = cw_ref[:, c0:c0 + sw]
        return cb_ref[:, c0:c0 + sw] + cw[0:1] * s2 + cw[1:2] * s1 + cw[2:3] * u

    acts = [_gated_act(conv(c0), conv(dff + c0)) for c0 in range(0, dff, sw)]
    xo = x1_ref[...] + _nn(jnp.concatenate(acts, axis=1), wd_ref[...])
    y_ref[...] = xo * lax.rsqrt(jnp.mean(xo * xo, axis=-1, keepdims=True) + RMS_EPS) * gf_ref[...]


def _ffn_prompt(h, x1, wup, cw, cb, wd, gf, *, seq, tm, sw):
    n, d = x1.shape
    dff = wd.shape[0]
    row = lambda i: (i, 0)
    return pl.pallas_call(
        functools.partial(_ffn_prompt_kernel, tiles_per_seq=seq // tm, dff=dff, sw=sw),
        grid=(n // tm,),
        in_specs=[pl.BlockSpec((tm, d), row), pl.BlockSpec((tm, d), row), _resident(wup.shape), _resident(cw.shape),
                  _resident(cb.shape), _resident(wd.shape), _resident(gf.shape)],
        out_specs=[pl.BlockSpec((tm, d), row), pl.BlockSpec((1, 8, 2 * dff), lambda i: (i, 0, 0))],
        out_shape=[jax.ShapeDtypeStruct((n, d), F32), jax.ShapeDtypeStruct((n // tm, 8, 2 * dff), F32)],
        scratch_shapes=[pltpu.VMEM((8, 2 * dff), F32)],
        compiler_params=_params(("arbitrary",)),
        name="ffn_prompt",
    )(h, x1, wup, cw, cb, wd, gf)


def _ffn_sample_kernel(h_ref, x1_ref, p0a_ref, p0g_ref, p1a_ref, p1g_ref, wa_ref, wg_ref, cwa_ref, cwg_ref,
                       cba_ref, cbg_ref, wd_ref, gf_ref, y_ref, ua_ref, ug_ref, acc_ref, *, n_chunks):
    c = pl.program_id(0)
    h = h_ref[...]

    def conv(w_ref, cw_ref, cb_ref, p0_ref, p1_ref, u_ref):
        u = _nn(h, w_ref[...])
        u_ref[...] = u
        cw = cw_ref[...]
        return cb_ref[...] + cw[0:1] * p0_ref[...] + cw[1:2] * p1_ref[...] + cw[2:3] * u

    ca = conv(wa_ref, cwa_ref, cba_ref, p0a_ref, p1a_ref, ua_ref)
    cg = conv(wg_ref, cwg_ref, cbg_ref, p0g_ref, p1g_ref, ug_ref)
    part = _nn(_gated_act(ca, cg), wd_ref[...])

    @pl.when(c == 0)
    def _():
        acc_ref[...] = part

    @pl.when(c > 0)
    def _():
        acc_ref[...] += part

    @pl.when(c == n_chunks - 1)
    def _():
        xo = x1_ref[...] + acc_ref[...]
        y_ref[...] = xo * lax.rsqrt(jnp.mean(xo * xo, axis=-1, keepdims=True) + RMS_EPS) * gf_ref[...]


def _ffn_sample(h, x1, prev, wup, cw, cb, wd, gf, *, tf):
    n, d = x1.shape
    dff = wd.shape[0]
    nc = dff // tf
    full = lambda c: (0, 0)
    col = lambda k: (lambda c: (0, k * nc + c))
    return pl.pallas_call(
        functools.partial(_ffn_sample_kernel, n_chunks=nc),
        grid=(nc,),
        in_specs=[pl.BlockSpec((n, d), full), pl.BlockSpec((n, d), full),
                  pl.BlockSpec((n, tf), col(0)), pl.BlockSpec((n, tf), col(1)),
                  pl.BlockSpec((n, tf), col(2)), pl.BlockSpec((n, tf), col(3)),
                  pl.BlockSpec((d, tf), col(0)), pl.BlockSpec((d, tf), col(1)),
                  pl.BlockSpec((CONV_W, tf), col(0)), pl.BlockSpec((CONV_W, tf), col(1)),
                  pl.BlockSpec((1, tf), col(0)), pl.BlockSpec((1, tf), col(1)),
                  pl.BlockSpec((tf, d), lambda c: (c, 0)), pl.BlockSpec((1, d), full)],
        out_specs=[pl.BlockSpec((n, d), full), pl.BlockSpec((n, tf), col(0)), pl.BlockSpec((n, tf), col(0))],
        out_shape=[jax.ShapeDtypeStruct((n, d), F32), jax.ShapeDtypeStruct((n, dff), F32),
                   jax.ShapeDtypeStruct((n, dff), F32)],
        scratch_shapes=[pltpu.VMEM((n, d), F32)],
        compiler_params=_params(("arbitrary",)),
        name="ffn_sample",
    )(h, x1, prev, prev, prev, prev, wup, wup, cw, cw, cb, cb, wd, gf)


ROWS = 16


def _pad_rows(x):
    return jnp.concatenate([x, jnp.zeros((ROWS - x.shape[0],) + x.shape[1:], x.dtype)], axis=0)


def _head_rows(q, nh):
    shape = (ROWS, q.shape[1])
    sub = lax.broadcasted_iota(jnp.int32, shape, 0)
    lane = lax.broadcasted_iota(jnp.int32, shape, 1)
    own = lane // HEAD_DIM == sub
    return jnp.where(own, jnp.broadcast_to(q, shape), 0.0), own


def _fox_decode_kernel(pt_ref, q_ref, kn_ref, vn_ref, lfn_ref, *refs, pg, nh, page):
    k_refs, v_refs, lf_refs = (refs[r * pg:(r + 1) * pg] for r in range(3))
    o_ref, m_ref, l_ref, acc_ref, carry_ref = refs[3 * pg:]
    j = pl.program_id(1)

    @pl.when(j == 0)
    def _():
        _attn_init(m_ref, l_ref, acc_ref)
        carry_ref[...] = jnp.zeros(carry_ref.shape, F32)

    qrows, own = _head_rows(q_ref[0], nh)
    q16 = qrows.astype(BF16)
    tri = (lax.broadcasted_iota(jnp.int32, (page, page), 0) <= lax.broadcasted_iota(jnp.int32, (page, page), 1))
    tri = tri.astype(F32)

    local = _nn(jnp.concatenate([_pad_rows(lf_refs[p][0]) for p in range(pg)], axis=0), tri, precision=HIGHEST)
    carry = carry_ref[...]
    scores = []
    for p in range(pg):
        loc = local[p * ROWS:(p + 1) * ROWS]
        scores.append(_nn(q16, k_refs[p][0].astype(BF16)) - (carry + loc))
        carry = carry + loc[:, page - 1:page]
    carry_ref[...] = carry
    s = jnp.concatenate(scores, axis=1)
    m_old = m_ref[...]
    m_new = jnp.maximum(m_old, jnp.max(s, axis=1, keepdims=True))
    alpha = jnp.exp(m_old - m_new)
    pr = jnp.exp(s - m_new)
    p16 = pr.astype(BF16)
    pv = _nt(p16[:, :page], v_refs[0][0].astype(BF16))
    for p in range(1, pg):
        pv = pv + _nt(p16[:, p * page:(p + 1) * page], v_refs[p][0].astype(BF16))
    l_ref[...] = alpha * l_ref[...] + jnp.sum(pr, axis=1, keepdims=True)
    acc_ref[...] = alpha * acc_ref[...] + pv
    m_ref[...] = m_new

    @pl.when(j == pl.num_programs(1) - 1)
    def _():
        lfn = _pad_rows(lfn_ref[0])
        s = jnp.sum(qrows * kn_ref[0], axis=1, keepdims=True) - (carry_ref[...] + lfn)
        m_old = m_ref[...]
        m_new = jnp.maximum(m_old, s)
        alpha = jnp.exp(m_old - m_new)
        p = jnp.exp(s - m_new)
        l = alpha * l_ref[...] + p
        acc = alpha * acc_ref[...] + p * vn_ref[0]
        o_ref[0] = jnp.sum(jnp.where(own, acc / l, 0.0), axis=0, keepdims=True)


def _fox_decode(pt, q, kn, vn, lfn, kc, vc, lfc, *, pg):
    db, _, aw = q.shape
    nh = aw // HEAD_DIM
    page = kc.shape[2]
    npg = pt.shape[1]
    vec = pl.BlockSpec((1, 1, aw), lambda b, j, pt: (b, 0, 0))
    pspec = lambda shape, p: pl.BlockSpec(shape, lambda b, j, pt: (pt[b, j * pg + p], 0, 0))
    in_specs = [vec, vec, vec, pl.BlockSpec((1, nh, 1), lambda b, j, pt: (b, 0, 0))]
    in_specs += [pspec((1, aw, page), p) for p in range(pg)] * 2
    in_specs += [pspec((1, nh, page), p) for p in range(pg)]
    return pl.pallas_call(
        functools.partial(_fox_decode_kernel, pg=pg, nh=nh, page=page),
        grid_spec=pltpu.PrefetchScalarGridSpec(
            num_scalar_prefetch=1, grid=(db, npg // pg), in_specs=in_specs, out_specs=vec,
            scratch_shapes=[pltpu.VMEM((ROWS, 1), F32), pltpu.VMEM((ROWS, 1), F32), pltpu.VMEM((ROWS, aw), F32),
                            pltpu.VMEM((ROWS, 1), F32)]),
        out_shape=jax.ShapeDtypeStruct((db, 1, aw), F32),
        compiler_params=_params(("arbitrary", "arbitrary")),
        name="fox_decode",
    )(pt, q, kn, vn, lfn, *([kc] * pg), *([vc] * pg), *([lfc] * pg))


def _moba_gate_accumulate(j, k_refs, km_ref, *, pg, page):
    per_block = MOBA_BLOCK // page
    lane = lax.broadcasted_iota(jnp.int32, km_ref.shape, 1)
    km = jnp.where(j == 0, 0.0, km_ref[...])
    for g in range(pg // per_block):
        tot = k_refs[g * per_block][0]
        for u in range(1, per_block):
            tot = tot + k_refs[g * per_block + u][0]
        mean = jnp.sum(tot, axis=1, keepdims=True) * (1.0 / MOBA_BLOCK)
        km = jnp.where(lane == j * (pg // per_block) + g, mean, km)
    km_ref[...] = km


def _moba_gate_select(j, n_steps, q_ref, sel_ref, km_ref, *, nh, n_blocks):
    @pl.when(j == n_steps - 1)
    def _():
        qrows, _ = _head_rows(q_ref[0], nh)
        gate = _nn(qrows, km_ref[...], precision=HIGHEST)
        lane = lax.broadcasted_iota(jnp.int32, gate.shape, 1)
        g = jnp.where(lane < n_blocks, gate, NEG_INF)
        lane_f = lane.astype(F32)
        out = jnp.zeros(gate.shape, F32)
        for t in range(MOBA_TOPK):
            m = jnp.max(g, axis=1, keepdims=True)
            idx = jnp.min(jnp.where(g == m, lane_f, 1e9), axis=1, keepdims=True)
            out = jnp.where(lane == t, idx, out)
            g = jnp.where(lane_f == idx, -3e38, g)
        sel_ref[0] = out[:nh].astype(jnp.int32)


def _moba_decode_kernel(pt_ref, sel_ref, q_ref, kn_ref, vn_ref, *refs, n_pages, hs):
    k_refs, v_refs, o_ref = refs[:hs * n_pages], refs[hs * n_pages:2 * hs * n_pages], refs[2 * hs * n_pages]
    qrows, own = _head_rows(q_ref[0, 0], hs)
    q16 = qrows.astype(BF16)

    def stacked(page_refs, r):
        return jnp.concatenate([page_refs[e * n_pages + r][0] for e in range(hs)], axis=0).astype(BF16)

    s_new = jnp.sum(qrows * kn_ref[0, 0], axis=1, keepdims=True)
    ss = [_nn(q16, stacked(k_refs, r)) for r in range(n_pages)]
    m = s_new
    for s in ss:
        m = jnp.maximum(m, jnp.max(s, axis=1, keepdims=True))
    p_new = jnp.exp(s_new - m)
    l = p_new
    acc = p_new * vn_ref[0, 0]
    for r, s in enumerate(ss):
        p = jnp.exp(s - m)
        l = l + jnp.sum(p, axis=1, keepdims=True)
        acc = acc + _nt(p.astype(BF16), stacked(v_refs, r))
    o_ref[0, 0] = jnp.sum(jnp.where(own, acc / l, 0.0), axis=0, keepdims=True)


def _moba_decode(pt, sel, q, kn, vn, kc, vc):
    db, ng, _, gw = q.shape
    hd = HEAD_DIM
    hs = gw // hd
    page = kc.shape[2]
    per_block = MOBA_BLOCK // page
    n_pages = MOBA_TOPK * per_block
    vec = pl.BlockSpec((1, 1, 1, gw), lambda b, g, pt, sel: (b, g, 0, 0))

    def pspec(e, r):
        t, u = divmod(r, per_block)
        return pl.BlockSpec(
            (1, hd, page),
            lambda b, g, pt, sel: (pt[b, sel[b, (g * hs + e) * MOBA_TOPK + t] * per_block + u], g * hs + e, 0))

    pages = [pspec(e, r) for e in range(hs) for r in range(n_pages)]
    return pl.pallas_call(
        functools.partial(_moba_decode_kernel, n_pages=n_pages, hs=hs),
        grid_spec=pltpu.PrefetchScalarGridSpec(
            num_scalar_prefetch=2, grid=(db, ng), in_specs=[vec, vec, vec] + pages + pages, out_specs=vec),
        out_shape=jax.ShapeDtypeStruct((db, ng, 1, gw), F32),
        compiler_params=_params(("arbitrary", "arbitrary")),
        name="moba_decode",
    )(pt, sel, q, kn, vn, *([kc] * (hs * n_pages)), *([vc] * (hs * n_pages)))


def _rope_tables(pos):
    inv = ROPE_THETA ** (-jnp.arange(HALF, dtype=F32) / HALF)
    ang = pos.astype(F32)[:, None] * inv[None, :]
    cos, sin, zero = jnp.cos(ang), jnp.sin(ang), jnp.zeros_like(ang)
    reps = LANES // HEAD_DIM
    return (jnp.tile(cos, (1, 2 * reps)), jnp.tile(jnp.concatenate([-sin, zero], axis=1), (1, reps)),
            jnp.tile(jnp.concatenate([zero, sin], axis=1), (1, reps)))


def _pick(n, candidates):
    for c in candidates:
        if n % c == 0:
            return c
    raise ValueError(f"no tile for {n}")


def kernel(x_prompt, x_sample, cache_fox_k, cache_fox_v, cache_fox_logf, cache_moba_k, cache_moba_v, state_ffn_conv, page_table, norm_mix_g, w_in, b_forget, w_fox_o, w_moba_o, w_out, norm_ffn_g, w_up, conv_w, conv_b, w_down, norm_final_g):
    batch, seq, d = x_prompt.shape
    db, dseq, _ = x_sample.shape
    depth, n_pool, page, nh, hd = cache_fox_k.shape
    aw = nh * hd
    dff = w_down.shape[1]
    n_pages = page_table.shape[1]
    past = n_pages * page
    assert depth == 1 and hd == HEAD_DIM and cache_moba_k.shape == cache_fox_k.shape and dseq == 1
    assert seq // MOBA_BLOCK <= HEAD_DIM and past // MOBA_BLOCK <= LANES
    assert seq % MOBA_BLOCK == 0 and past % MOBA_BLOCK == 0 and past // MOBA_BLOCK >= MOBA_TOPK
    assert MOBA_BLOCK % page == 0 and aw % LANES == 0 and nh <= 8

    n = batch * seq
    tm = _pick(seq, (512, 256, 128))
    t_attn = _pick(seq, (1024, 512, MOBA_BLOCK))
    tf = _pick(dff, (1408, 1024, 512, 256, 128))
    pg = _pick(n_pages, (16, 8, 4, 2))
    tables_p = _rope_tables(jnp.arange(seq))
    tables_s = _rope_tables(jnp.full((db,), past))

    xp = x_prompt.reshape(n, d)
    xs = x_sample.reshape(db, d)
    feat_major = lambda c: jnp.transpose(c, (0, 1, 3, 4, 2)).reshape(depth * n_pool, aw, page)
    fk, fv, mk, mv = (feat_major(c) for c in (cache_fox_k, cache_fox_v, cache_moba_k, cache_moba_v))
    fl = jnp.transpose(cache_fox_logf, (0, 1, 3, 2)).reshape(depth * n_pool, nh, page)
    pos_major = lambda t: jnp.transpose(t.reshape(batch, nh, -1, seq), (0, 3, 1, 2))
    sizes = (aw, aw, aw, nh, aw, aw, aw, d, d)
    offs = [sum(sizes[:i]) for i in range(len(sizes) + 1)]

    p_new, s_new = [], []
    for l in range(depth):
        cols = [w_in[l][:, offs[i]:offs[i + 1]] for i in range(len(sizes))]
        qa, ka, va, fa, qb, kb, vb, ga, gb = cols
        w_cat = jnp.concatenate([qa * ATTN_SCALE, ka, va, qb * ATTN_SCALE, kb, vb, ga, gb,
                                 jnp.pad(fa, ((0, 0), (0, LANES - nh)))], axis=1).astype(BF16)
        bf = jnp.pad(b_forget[l].astype(F32), (0, LANES - nh)).reshape(1, LANES)
        g_mix = norm_mix_g[l].reshape(1, d)
        g_ffn = norm_ffn_g[l].reshape(1, d)
        wfo, wmo, wout = w_fox_o[l].astype(BF16), w_moba_o[l].astype(BF16), w_out[l].astype(BF16)
        wup, wd = w_up[l].astype(BF16), w_down[l].astype(BF16)
        cw, cb = conv_w[l], conv_b[l].reshape(1, 2 * dff)
        gf = norm_final_g.reshape(1, d)
        pt = page_table + l * n_pool

        qa_p, ka_p, va_p, qb_p, kb_p, vb_p, ga_p, gb_p, lf_p = _inproj(
            xp, g_mix, w_cat, bf, *tables_p, tm=tm, aw=aw, nh=nh, seq=seq)
        qa_s, ka_s, va_s, qb_s, kb_s, vb_s, ga_s, gb_s, lf_s = _inproj(
            xs, g_mix, w_cat, bf, *tables_s, tm=db, aw=aw, nh=nh)
        r1 = lambda t: t.reshape(db, 1, aw)

        oa_p = _fox_prompt(qa_p, ka_p, va_p, lf_p, batch=batch, seq=seq, t=t_attn)
        ob_p, sel = _moba_prompt(pt, qb_p, kb_p, vb_p, r1(qb_s), mk, batch=batch, seq=seq, t=t_attn)
        x1_p, h2_p = _merge(oa_p, ob_p, ga_p, gb_p, xp, wfo, wmo, wout, g_ffn, tm=tm)
        xp, tails = _ffn_prompt(h2_p, x1_p, wup, cw, cb, wd, gf, seq=seq, tm=tm, sw=_pick(dff, (256, LANES)))
        tps = seq // tm
        tail_p = tails[tps - 1::tps, 8 - (CONV_W - 1):]
        p_new.append((pos_major(ka_p), pos_major(va_p), jnp.transpose(lf_p, (0, 2, 1)), pos_major(kb_p),
                      pos_major(vb_p), tail_p))

        hs = _pick(nh, (4, 2, 1))
        r3 = lambda t: t.reshape(db, nh // hs, 1, hs * hd)
        oa_s = _fox_decode(pt, r1(qa_s), r1(ka_s), r1(va_s), lf_s.reshape(db, nh, 1), fk, fv, fl, pg=pg)
        sel = sel[:, :, :MOBA_TOPK].reshape(db, nh * MOBA_TOPK)
        ob_s = _moba_decode(pt, sel, r3(qb_s), r3(kb_s), r3(vb_s), mk, mv)
        x1_s, h2_s = _merge(oa_s.reshape(db, aw), ob_s.reshape(db, aw), ga_s, gb_s, xs, wfo, wmo, wout, g_ffn, tm=db)
        prev = state_ffn_conv[l]
        xs, ua, ug = _ffn_sample(h2_s, x1_s, prev.reshape(db, (CONV_W - 1) * 2 * dff), wup, cw, cb, wd, gf, tf=tf)
        tail_s = jnp.concatenate([prev[:, 1:], jnp.concatenate([ua, ug], axis=-1)[:, None]], axis=1)
        s_new.append((ka_s.reshape(db, 1, nh, hd), va_s.reshape(db, 1, nh, hd), lf_s.reshape(db, 1, nh),
                      kb_s.reshape(db, 1, nh, hd), vb_s.reshape(db, 1, nh, hd), tail_s))

    p_out = [jnp.stack(z) for z in zip(*p_new)]
    s_out = [jnp.stack(z) for z in zip(*s_new)]
    return (xp.reshape(batch, seq, d), xs.reshape(db, 1, d), *p_out, *s_out)
```

```python
import functools

import jax
import jax.numpy as jnp
from jax import lax
from jax.experimental import pallas as pl
from jax.experimental.pallas import tpu as pltpu

HEAD_DIM = 64
HALF = HEAD_DIM // 2
LANES = 128
MOBA_BLOCK = 256
MOBA_TOPK = 3
CONV_W = 3
ROPE_THETA = 10000.0
RMS_EPS = 1e-6
NEG_INF = -1e30
MASK_BIAS = -2e30
ATTN_SCALE = HEAD_DIM ** -0.5
LOG2E = 1.4426950408889634
VMEM_LIMIT = 56 * 1024 * 1024

F32 = jnp.float32
BF16 = jnp.bfloat16
HIGHEST = lax.Precision.HIGHEST


def _nt(a, b, precision=None):
    return lax.dot_general(a, b, (((1,), (1,)), ((), ())), preferred_element_type=F32, precision=precision)


def _nn(a, b, precision=None):
    return jnp.dot(a, b, preferred_element_type=F32, precision=precision)


def _params(sem):
    return pltpu.CompilerParams(dimension_semantics=sem, vmem_limit_bytes=VMEM_LIMIT)


def _resident(shape):
    nd = len(shape)
    return pl.BlockSpec(shape, lambda *_: (0,) * nd, pipeline_mode=pl.Buffered(1))


def _inproj_kernel(x_ref, g_ref, w_ref, bf_ref, cos_ref, sina_ref, sinb_ref,
                   qa_ref, ka_ref, va_ref, qb_ref, kb_ref, vb_ref, ga_ref, gb_ref, lf_ref,
                   *, aw, d_model, nh, position_minor):
    x = x_ref[...]
    h = (x * lax.rsqrt(jnp.mean(x * x, axis=-1, keepdims=True) + RMS_EPS) * g_ref[...]).astype(BF16)

    def proj(c0, width):
        return _nn(h, w_ref[:, c0:c0 + width])

    def put(o_ref, val):
        if position_minor:
            o_ref[0] = val.T
        else:
            o_ref[...] = val

    cos, sina, sinb = cos_ref[...], sina_ref[...], sinb_ref[...]

    def rope(t):
        out = []
        for s in range(aw // LANES):
            xs = t[:, s * LANES:(s + 1) * LANES]
            out.append(xs * cos + pltpu.roll(xs, LANES - HALF, 1) * sina + pltpu.roll(xs, HALF, 1) * sinb)
        return jnp.concatenate(out, axis=1)

    qa_ref[...] = proj(0, aw)
    put(ka_ref, proj(aw, aw))
    put(va_ref, proj(2 * aw, aw))
    qb_ref[...] = rope(proj(3 * aw, aw))
    put(kb_ref, rope(proj(4 * aw, aw)))
    put(vb_ref, proj(5 * aw, aw))
    ga_ref[...] = jax.nn.sigmoid(proj(6 * aw, d_model)).astype(ga_ref.dtype)
    gb_ref[...] = jax.nn.sigmoid(proj(6 * aw + d_model, d_model)).astype(gb_ref.dtype)

    f = proj(6 * aw + 2 * d_model, LANES) + bf_ref[...]
    lsig = -(jnp.maximum(-f, 0.0) + jnp.log1p(jnp.exp(-jnp.abs(f))))
    if position_minor:
        lf_ref[0] = lsig.T[:nh, :]
    else:
        lf_ref[...] = lsig[:, :nh]


def _inproj(x, g, w, bf, cos, sina, sinb, *, tm, aw, nh, seq=None):
    n, d = x.shape
    nt = n // tm
    tb = cos.shape[0] // tm
    row = lambda i: (i, 0)
    tbl = lambda i: (i % tb, 0)
    nat = (jax.ShapeDtypeStruct((n, aw), F32), pl.BlockSpec((tm, aw), row))
    gate = (jax.ShapeDtypeStruct((n, d), BF16), pl.BlockSpec((tm, d), row))
    if seq is None:
        kv = nat
        lf = (jax.ShapeDtypeStruct((n, nh), F32), pl.BlockSpec((tm, nh), row))
    else:
        tps = seq // tm
        pm = lambda i: (i // tps, 0, i % tps)
        kv = (jax.ShapeDtypeStruct((n // seq, aw, seq), F32), pl.BlockSpec((1, aw, tm), pm))
        lf = (jax.ShapeDtypeStruct((n // seq, nh, seq), F32), pl.BlockSpec((1, nh, tm), pm))
    outs = [nat, kv, kv, nat, kv, kv, gate, gate, lf]
    return pl.pallas_call(
        functools.partial(_inproj_kernel, aw=aw, d_model=d, nh=nh, position_minor=seq is not None),
        grid=(nt,),
        in_specs=[pl.BlockSpec((tm, d), row), _resident(g.shape), _resident(w.shape), _resident(bf.shape),
                  pl.BlockSpec((tm, LANES), tbl), pl.BlockSpec((tm, LANES), tbl), pl.BlockSpec((tm, LANES), tbl)],
        out_specs=[o[1] for o in outs],
        out_shape=[o[0] for o in outs],
        compiler_params=_params(("arbitrary",)),
        name="inproj",
    )(x, g, w, bf, cos, sina, sinb)


def _attn_init(m_ref, l_ref, acc_ref):
    m_ref[...] = jnp.full(m_ref.shape, NEG_INF, F32)
    l_ref[...] = jnp.zeros(l_ref.shape, F32)
    acc_ref[...] = jnp.zeros(acc_ref.shape, F32)


def _flash_causal(i, qts, k_refs, vt_ref, o_ref, *, t, side_work=None):
    def block(j, carry, masked=False):
        off = pl.multiple_of(j * t, t)
        out = []
        for e in range(2):
            m_old, l_old, acc = carry[e]
            s = _nn(k_refs[e][pl.ds(off, t), :], qts[e])
            if masked:
                kpos = lax.broadcasted_iota(jnp.int32, (t, t), 0)
                qpos = lax.broadcasted_iota(jnp.int32, (t, t), 1)
                s = jnp.where(kpos <= qpos, s, NEG_INF)
            m_new = jnp.maximum(m_old, jnp.max(s, axis=0, keepdims=True))
            alpha = jnp.exp2(m_old - m_new)
            p = jnp.exp2(s - m_new)
            l_new = alpha * l_old + jnp.sum(p, axis=0, keepdims=True)
            vt = vt_ref[e * HEAD_DIM:(e + 1) * HEAD_DIM, pl.ds(off, t)]
            out.append((m_new, l_new, alpha * acc + _nn(vt, p.astype(BF16))))
        return tuple(out)

    init = tuple((jnp.full((1, t), NEG_INF, F32), jnp.zeros((1, t), F32), jnp.zeros((HEAD_DIM, t), F32))
                 for _ in range(2))
    carry = lax.fori_loop(0, i // 2, lambda jj, c: block(2 * jj + 1, block(2 * jj, c)), init)
    carry = lax.cond(i % 2 == 1, lambda c: block(i - 1, c), lambda c: c, carry)
    if side_work is not None:
        side_work()
    carry = block(i, carry, masked=True)
    o_ref[...] = jnp.concatenate([acc / l for _, l, acc in carry], axis=0).T.astype(o_ref.dtype)


def _split3(x):
    hi = x.astype(BF16).astype(F32)
    mid = (x - hi).astype(BF16).astype(F32)
    lo = (x - hi - mid).astype(BF16).astype(F32)
    return hi, mid, lo


def _fox_prompt_kernel(q_ref, k_ref, v_ref, lft_ref, o_ref, ka16, kb16, vt16, *, t, seq):
    i = pl.program_id(2)

    @pl.when(i == 0)
    def _():
        c = lft_ref[0, 0]
        pos = lax.broadcasted_iota(jnp.int32, c.shape, 1)
        sh = 1
        while sh < seq:
            c = c + jnp.where(pos >= sh, pltpu.roll(c, sh, 1), 0.0)
            sh *= 2
        sub = lax.broadcasted_iota(jnp.int32, (8, seq), 0)
        rows = []
        for e in range(2):
            parts = _split3(-LOG2E * c[e:e + 1])
            rows.append(sum(jnp.where(sub == r, jnp.broadcast_to(part, (8, seq)), 0.0)
                            for r, part in enumerate(parts)))
        fill = jnp.zeros((HEAD_DIM - 8, seq), F32)
        decay = jnp.concatenate([rows[1], fill, rows[0], fill], axis=0).T
        k = k_ref[0].T
        lane = lax.broadcasted_iota(jnp.int32, k.shape, 1)
        ka16[...] = jnp.where(lane < HEAD_DIM, k, decay).astype(BF16)
        kb16[...] = jnp.where(lane >= HEAD_DIM, k, decay).astype(BF16)
        vt16[...] = v_ref[0].astype(BF16)

    qt = (q_ref[...] * LOG2E).T
    ones = jnp.where(lax.broadcasted_iota(jnp.int32, (8, t), 0) < 3, 1.0, 0.0)
    fill = jnp.zeros((HEAD_DIM - 8, t), F32)
    qts = (jnp.concatenate([qt[:HEAD_DIM], ones, fill], axis=0).astype(BF16),
           jnp.concatenate([ones, fill, qt[HEAD_DIM:]], axis=0).astype(BF16))
    _flash_causal(i, qts, (ka16, kb16), vt16, o_ref, t=t)


def _fox_prompt(q, k, v, lft, *, batch, seq, t):
    n, aw = q.shape
    hp = aw // LANES
    nq = seq // t
    lft4 = lft.reshape(batch, hp, 2, seq)
    qmap = lambda b, h, i: (b * nq + i, h)
    kmap = lambda b, h, i: (b, h, 0)
    return pl.pallas_call(
        functools.partial(_fox_prompt_kernel, t=t, seq=seq),
        grid=(batch, hp, nq),
        in_specs=[pl.BlockSpec((t, LANES), qmap), pl.BlockSpec((1, LANES, seq), kmap), pl.BlockSpec((1, LANES, seq), kmap),
                  pl.BlockSpec((1, 1, 2, seq), lambda b, h, i: (b, h, 0, 0))],
        out_specs=pl.BlockSpec((t, LANES), qmap),
        out_shape=jax.ShapeDtypeStruct((n, aw), BF16),
        scratch_shapes=[pltpu.VMEM((seq, LANES), BF16), pltpu.VMEM((seq, LANES), BF16), pltpu.VMEM((LANES, seq), BF16)],
        compiler_params=_params(("arbitrary", "arbitrary", "arbitrary")),
        name="fox_prompt",
    )(q, k, v, lft4)


def _block_bias(gate, limit):
    n = lax.broadcasted_iota(jnp.int32, gate.shape, 0)
    nf = n.astype(F32)
    g = jnp.where(n < limit, gate, NEG_INF)
    sel = jnp.zeros(gate.shape, F32)
    for _ in range(MOBA_TOPK):
        m = jnp.max(g, axis=0, keepdims=True)
        idx = jnp.min(jnp.where(g == m, nf, 1e9), axis=0, keepdims=True)
        hit = nf == idx
        sel = jnp.where(hit & (idx < limit), 1.0, sel)
        g = jnp.where(hit, -3e38, g)
    return jnp.where((n < limit) & (sel < 0.5), MASK_BIAS, 0.0)


def _moba_prompt_kernel(pt_ref, q_ref, k_ref, v_ref, qs_ref, *refs, seq, nbp, t, pg, nh, page, n_blocks, sps):
    mk_refs = refs[:pg]
    o_ref, sel_ref, ka16, kb16, vt16, km_ref, kms_ref = refs[pg:]
    bs = MOBA_BLOCK
    nb = seq // bs
    i = pl.program_id(2)
    step = (pl.program_id(0) * pl.num_programs(1) + pl.program_id(1)) * pl.num_programs(2) + i

    @pl.when(step == 0)
    def _():
        kms_ref[...] = jnp.zeros(kms_ref.shape, F32)

    @pl.when(i == 0)
    def _():
        k = k_ref[0].T
        lane = lax.broadcasted_iota(jnp.int32, k.shape, 1)
        blk = lax.broadcasted_iota(jnp.int32, k.shape, 0) // bs
        ka16[...] = jnp.where(lane < HEAD_DIM, k, jnp.where(lane - HEAD_DIM == blk, 1.0, 0.0)).astype(BF16)
        kb16[...] = jnp.where(lane >= HEAD_DIM, k, jnp.where(lane == blk, 1.0, 0.0)).astype(BF16)
        vt16[...] = v_ref[0].astype(BF16)
        km_ref[...] = jnp.zeros(km_ref.shape, F32)
        for n in range(nb):
            km_ref[n:n + 1, :] = jnp.sum(k[n * bs:(n + 1) * bs, :], axis=0, keepdims=True) * (1.0 / bs)

    qt = q_ref[...].T
    km = km_ref[...]
    lane = lax.broadcasted_iota(jnp.int32, km.shape, 1)
    fill = jnp.zeros((HEAD_DIM - nbp, t), F32)
    own = (i * t + lax.broadcasted_iota(jnp.int32, (1, t), 1)) // bs
    bias_a = _block_bias(_nn(jnp.where(lane < HEAD_DIM, km, 0.0), qt, precision=HIGHEST), own)
    bias_b = _block_bias(_nn(jnp.where(lane >= HEAD_DIM, km, 0.0), qt, precision=HIGHEST), own)
    qs = qt * LOG2E
    qts = (jnp.concatenate([qs[:HEAD_DIM], bias_a, fill], axis=0).astype(BF16),
           jnp.concatenate([bias_b, fill, qs[HEAD_DIM:]], axis=0).astype(BF16))
    scan = functools.partial(_moba_gate_accumulate, step % sps, mk_refs, kms_ref, pg=pg, page=page)
    _flash_causal(i, qts, (ka16, kb16), vt16, o_ref, t=t, side_work=scan)
    _moba_gate_select(step % sps, sps, qs_ref, sel_ref, kms_ref, nh=nh, n_blocks=n_blocks)


def _moba_prompt(pt, q, k, v, qs, mkc, *, batch, seq, t):
    n, aw = q.shape
    db, npg = pt.shape
    page = mkc.shape[2]
    hp = aw // LANES
    nh = aw // HEAD_DIM
    nq = seq // t
    nbp = -(-(seq // MOBA_BLOCK) // 8) * 8
    steps = batch * hp * nq
    assert steps % db == 0 and npg % (steps // db) == 0, "sample-row scan does not tile this call's grid"
    sps = steps // db
    pg = npg // sps
    assert pg % (MOBA_BLOCK // page) == 0
    qmap = lambda b, h, i, pt: (b * nq + i, h)
    kmap = lambda b, h, i, pt: (b, h, 0)
    lin = lambda b, h, i: (b * hp + h) * nq + i
    smap = lambda b, h, i, pt: (lin(b, h, i) // sps, 0, 0)
    pmap = lambda p: (lambda b, h, i, pt: (pt[lin(b, h, i) // sps, (lin(b, h, i) % sps) * pg + p], 0, 0))
    in_specs = [pl.BlockSpec((t, LANES), qmap), pl.BlockSpec((1, LANES, seq), kmap), pl.BlockSpec((1, LANES, seq), kmap),
                pl.BlockSpec((1, 1, aw), smap)]
    in_specs += [pl.BlockSpec((1, aw, page), pmap(p)) for p in range(pg)]
    return pl.pallas_call(
        functools.partial(_moba_prompt_kernel, seq=seq, nbp=nbp, t=t, pg=pg, nh=nh, page=page,
                          n_blocks=npg * page // MOBA_BLOCK, sps=sps),
        grid_spec=pltpu.PrefetchScalarGridSpec(
            num_scalar_prefetch=1, grid=(batch, hp, nq), in_specs=in_specs,
            out_specs=[pl.BlockSpec((t, LANES), qmap), pl.BlockSpec((1, nh, LANES), smap)],
            scratch_shapes=[pltpu.VMEM((seq, LANES), BF16), pltpu.VMEM((seq, LANES), BF16),
                            pltpu.VMEM((LANES, seq), BF16), pltpu.VMEM((nbp, LANES), F32),
                            pltpu.VMEM((aw, LANES), F32)]),
        out_shape=[jax.ShapeDtypeStruct((n, aw), BF16),
                   jax.ShapeDtypeStruct((db, nh, LANES), jnp.int32)],
        compiler_params=_params(("arbitrary", "arbitrary", "arbitrary")),
        name="moba_prompt",
    )(pt, q, k, v, qs, *([mkc] * pg))


def _merge_kernel(oa_ref, ob_ref, ga_ref, gb_ref, x_ref, wfo_ref, wmo_ref, wout_ref, g_ref, x1_ref, h_ref):
    pa = _nn(oa_ref[...].astype(BF16), wfo_ref[...])
    pb = _nn(ob_ref[...].astype(BF16), wmo_ref[...])
    mix = ga_ref[...] * pa + gb_ref[...] * pb
    x1 = x_ref[...] + _nn(mix.astype(BF16), wout_ref[...])
    x1_ref[...] = x1
    h_ref[...] = (x1 * lax.rsqrt(jnp.mean(x1 * x1, axis=-1, keepdims=True) + RMS_EPS) * g_ref[...]).astype(BF16)


def _merge(oa, ob, ga, gb, x, wfo, wmo, wout, g, *, tm):
    n, d = x.shape
    aw = oa.shape[1]
    row = lambda i: (i, 0)
    return pl.pallas_call(
        _merge_kernel,
        grid=(n // tm,),
        in_specs=[pl.BlockSpec((tm, aw), row), pl.BlockSpec((tm, aw), row), pl.BlockSpec((tm, d), row),
                  pl.BlockSpec((tm, d), row), pl.BlockSpec((tm, d), row),
                  _resident(wfo.shape), _resident(wmo.shape), _resident(wout.shape), _resident(g.shape)],
        out_specs=[pl.BlockSpec((tm, d), row), pl.BlockSpec((tm, d), row)],
        out_shape=[jax.ShapeDtypeStruct((n, d), F32), jax.ShapeDtypeStruct((n, d), BF16)],
        compiler_params=_params(("arbitrary",)),
        name="merge",
    )(oa, ob, ga, gb, x, wfo, wmo, wout, g)


def _gated_act(ca, cg):
    gelu = 0.5 * ca * (1.0 + lax.erf(ca * (2.0 ** -0.5)))
    return (gelu * cg).astype(BF16)


def _ffn_prompt_kernel(h_ref, x1_ref, wup_ref, cw_ref, cb_ref, wd_ref, gf_ref, y_ref, tail_ref, prev_ref,
                       *, tiles_per_seq, dff, sw):
    i = pl.program_id(0)
    h = h_ref[...]
    tm = h.shape[0]
    row = lax.broadcasted_iota(jnp.int32, (8, sw), 0)

    @pl.when(i % tiles_per_seq == 0)
    def _():
        prev_ref[...] = jnp.zeros(prev_ref.shape, F32)

    def conv(c0):
        u = _nn(h, wup_ref[:, c0:c0 + sw])
        last = u[tm - 8:, :]
        tail_ref[0, :, c0:c0 + sw] = last
        prev = prev_ref[:, c0:c0 + sw]
        prev_ref[:, c0:c0 + sw] = last
        s1 = pltpu.roll(u, 1, 0)
        s2 = pltpu.roll(u, 2, 0)
        s1 = jnp.concatenate([jnp.where(row < 1, pltpu.roll(prev, 1, 0), s1[:8]), s1[8:]], axis=0)
        s2 = jnp.concatenate([jnp.where(row < 2, pltpu.roll(prev, 2, 0), s2[:8]), s2[8:]], axis=0)
        cw = cw_ref[:, c0:c0 + sw]
        return cb_ref[:, c0:c0 + sw] + cw[0:1] * s2 + cw[1:2] * s1 + cw[2:3] * u

    acts = [_gated_act(conv(c0), conv(dff + c0)) for c0 in range(0, dff, sw)]
    xo = x1_ref[...] + _nn(jnp.concatenate(acts, axis=1), wd_ref[...])
    y_ref[...] = xo * lax.rsqrt(jnp.mean(xo * xo, axis=-1, keepdims=True) + RMS_EPS) * gf_ref[...]


def _ffn_prompt(h, x1, wup, cw, cb, wd, gf, *, seq, tm, sw):
    n, d = x1.shape
    dff = wd.shape[0]
    row = lambda i: (i, 0)
    return pl.pallas_call(
        functools.partial(_ffn_prompt_kernel, tiles_per_seq=seq // tm, dff=dff, sw=sw),
        grid=(n // tm,),
        in_specs=[pl.BlockSpec((tm, d), row), pl.BlockSpec((tm, d), row), _resident(wup.shape), _resident(cw.shape),
                  _resident(cb.shape), _resident(wd.shape), _resident(gf.shape)],
        out_specs=[pl.BlockSpec((tm, d), row), pl.BlockSpec((1, 8, 2 * dff), lambda i: (i, 0, 0))],
        out_shape=[jax.ShapeDtypeStruct((n, d), F32), jax.ShapeDtypeStruct((n // tm, 8, 2 * dff), F32)],
        scratch_shapes=[pltpu.VMEM((8, 2 * dff), F32)],
        compiler_params=_params(("arbitrary",)),
        name="ffn_prompt",
    )(h, x1, wup, cw, cb, wd, gf)


def _ffn_sample_kernel(h_ref, x1_ref, p0a_ref, p0g_ref, p1a_ref, p1g_ref, wa_ref, wg_ref, cwa_ref, cwg_ref,
                       cba_ref, cbg_ref, wd_ref, gf_ref, y_ref, ua_ref, ug_ref, acc_ref, *, n_chunks):
    c = pl.program_id(0)
    h = h_ref[...]

    def conv(w_ref, cw_ref, cb_ref, p0_ref, p1_ref, u_ref):
        u = _nn(h, w_ref[...])
        u_ref[...] = u
        cw = cw_ref[...]
        return cb_ref[...] + cw[0:1] * p0_ref[...] + cw[1:2] * p1_ref[...] + cw[2:3] * u

    ca = conv(wa_ref, cwa_ref, cba_ref, p0a_ref, p1a_ref, ua_ref)
    cg = conv(wg_ref, cwg_ref, cbg_ref, p0g_ref, p1g_ref, ug_ref)
    part = _nn(_gated_act(ca, cg), wd_ref[...])

    @pl.when(c == 0)
    def _():
        acc_ref[...] = part

    @pl.when(c > 0)
    def _():
        acc_ref[...] += part

    @pl.when(c == n_chunks - 1)
    def _():
        xo = x1_ref[...] + acc_ref[...]
        y_ref[...] = xo * lax.rsqrt(jnp.mean(xo * xo, axis=-1, keepdims=True) + RMS_EPS) * gf_ref[...]


def _ffn_sample(h, x1, prev, wup, cw, cb, wd, gf, *, tf):
    n, d = x1.shape
    dff = wd.shape[0]
    nc = dff // tf
    full = lambda c: (0, 0)
    col = lambda k: (lambda c: (0, k * nc + c))
    return pl.pallas_call(
        functools.partial(_ffn_sample_kernel, n_chunks=nc),
        grid=(nc,),
        in_specs=[pl.BlockSpec((n, d), full), pl.BlockSpec((n, d), full),
                  pl.BlockSpec((n, tf), col(0)), pl.BlockSpec((n, tf), col(1)),
                  pl.BlockSpec((n, tf), col(2)), pl.BlockSpec((n, tf), col(3)),
                  pl.BlockSpec((d, tf), col(0)), pl.BlockSpec((d, tf), col(1)),
                  pl.BlockSpec((CONV_W, tf), col(0)), pl.BlockSpec((CONV_W, tf), col(1)),
                  pl.BlockSpec((1, tf), col(0)), pl.BlockSpec((1, tf), col(1)),
                  pl.BlockSpec((tf, d), lambda c: (c, 0)), pl.BlockSpec((1, d), full)],
        out_specs=[pl.BlockSpec((n, d), full), pl.BlockSpec((n, tf), col(0)), pl.BlockSpec((n, tf), col(0))],
        out_shape=[jax.ShapeDtypeStruct((n, d), F32), jax.ShapeDtypeStruct((n, dff), F32),
                   jax.ShapeDtypeStruct((n, dff), F32)],
        scratch_shapes=[pltpu.VMEM((n, d), F32)],
        compiler_params=_params(("arbitrary",)),
        name="ffn_sample",
    )(h, x1, prev, prev, prev, prev, wup, wup, cw, cw, cb, cb, wd, gf)


ROWS = 16


def _pad_rows(x):
    return jnp.concatenate([x, jnp.zeros((ROWS - x.shape[0],) + x.shape[1:], x.dtype)], axis=0)


def _head_rows(q, nh):
    shape = (ROWS, q.shape[1])
    sub = lax.broadcasted_iota(jnp.int32, shape, 0)
    lane = lax.broadcasted_iota(jnp.int32, shape, 1)
    own = lane // HEAD_DIM == sub
    return jnp.where(own, jnp.broadcast_to(q, shape), 0.0), own


def _fox_decode_kernel(pt_ref, q_ref, qrep_ref, kn_ref, vn_ref, lfn_ref, *refs, pg, nh, page):
    k_refs, v_refs, lf_refs = (refs[r * pg:(r + 1) * pg] for r in range(3))
    o_ref, m_ref, l_ref, acc_ref, carry_ref = refs[3 * pg:]
    j = pl.program_id(1)
    aw = nh * HEAD_DIM

    @pl.when(j == 0)
    def _():
        _attn_init(m_ref, l_ref, acc_ref)
        carry_ref[...] = jnp.zeros(carry_ref.shape, F32)

    def per_feature(x):
        return jnp.broadcast_to(x[:, None, :], (nh, HEAD_DIM, x.shape[1])).reshape(aw, x.shape[1])

    qb = qrep_ref[0]
    tri = (lax.broadcasted_iota(jnp.int32, (page, page), 0) <= lax.broadcasted_iota(jnp.int32, (page, page), 1))
    tri = tri.astype(F32)

    local = _nn(jnp.concatenate([lf_refs[p][0] for p in range(pg)], axis=0), tri, precision=HIGHEST)
    carry = carry_ref[...]
    scores = []
    for p in range(pg):
        loc = local[p * nh:(p + 1) * nh]
        qk = jnp.sum((k_refs[p][0] * qb).reshape(nh, HEAD_DIM, page), axis=1)
        scores.append(qk - (carry + loc))
        carry = carry + loc[:, page - 1:page]
    carry_ref[...] = carry
    s = jnp.concatenate(scores, axis=1)
    m_old = m_ref[...]
    m_new = jnp.maximum(m_old, jnp.max(s, axis=1, keepdims=True))
    alpha = jnp.exp(m_old - m_new)
    pr = jnp.exp(s - m_new)
    acc = per_feature(alpha) * acc_ref[...]
    for p in range(pg):
        acc = acc + per_feature(pr[:, p * page:(p + 1) * page]) * v_refs[p][0]
    l_ref[...] = alpha * l_ref[...] + jnp.sum(pr, axis=1, keepdims=True)
    acc_ref[...] = acc
    m_ref[...] = m_new

    @pl.when(j == pl.num_programs(1) - 1)
    def _():
        pv = _nt(jnp.ones((8, page), F32), acc_ref[...], precision=HIGHEST)[0:1]
        qrows, own = _head_rows(q_ref[0], nh)
        pv = jnp.where(own, jnp.broadcast_to(pv, own.shape), 0.0)
        s = jnp.sum(qrows * kn_ref[0], axis=1, keepdims=True) - _pad_rows(carry_ref[...] + lfn_ref[0])
        m_old = _pad_rows(m_ref[...])
        m_new = jnp.maximum(m_old, s)
        alpha = jnp.exp(m_old - m_new)
        p = jnp.exp(s - m_new)
        l = alpha * _pad_rows(l_ref[...]) + p
        acc = alpha * pv + p * vn_ref[0]
        o_ref[0] = jnp.sum(jnp.where(own, acc / l, 0.0), axis=0, keepdims=True)


def _fox_decode(pt, q, kn, vn, lfn, kc, vc, lfc, *, pg):
    db, _, aw = q.shape
    nh = aw // HEAD_DIM
    page = kc.shape[2]
    npg = pt.shape[1]
    qrep = jnp.broadcast_to(q.reshape(db, aw, 1), (db, aw, page))
    vec = pl.BlockSpec((1, 1, aw), lambda b, j, pt: (b, 0, 0))
    pspec = lambda shape, p: pl.BlockSpec(shape, lambda b, j, pt: (pt[b, j * pg + p], 0, 0))
    in_specs = [vec, pl.BlockSpec((1, aw, page), lambda b, j, pt: (b, 0, 0)), vec, vec,
                pl.BlockSpec((1, nh, 1), lambda b, j, pt: (b, 0, 0))]
    in_specs += [pspec((1, aw, page), p) for p in range(pg)] * 2
    in_specs += [pspec((1, nh, page), p) for p in range(pg)]
    return pl.pallas_call(
        functools.partial(_fox_decode_kernel, pg=pg, nh=nh, page=page),
        grid_spec=pltpu.PrefetchScalarGridSpec(
            num_scalar_prefetch=1, grid=(db, npg // pg), in_specs=in_specs, out_specs=vec,
            scratch_shapes=[pltpu.VMEM((nh, 1), F32), pltpu.VMEM((nh, 1), F32), pltpu.VMEM((aw, page), F32),
                            pltpu.VMEM((nh, 1), F32)]),
        out_shape=jax.ShapeDtypeStruct((db, 1, aw), F32),
        compiler_params=_params(("arbitrary", "arbitrary")),
        name="fox_decode",
    )(pt, q, qrep, kn, vn, lfn, *([kc] * pg), *([vc] * pg), *([lfc] * pg))


def _moba_gate_accumulate(j, k_refs, km_ref, *, pg, page):
    per_block = MOBA_BLOCK // page
    lane = lax.broadcasted_iota(jnp.int32, km_ref.shape, 1)
    km = jnp.where(j == 0, 0.0, km_ref[...])
    for g in range(pg // per_block):
        tot = k_refs[g * per_block][0]
        for u in range(1, per_block):
            tot = tot + k_refs[g * per_block + u][0]
        mean = jnp.sum(tot, axis=1, keepdims=True) * (1.0 / MOBA_BLOCK)
        km = jnp.where(lane == j * (pg // per_block) + g, mean, km)
    km_ref[...] = km


def _moba_gate_select(j, n_steps, q_ref, sel_ref, km_ref, *, nh, n_blocks):
    @pl.when(j == n_steps - 1)
    def _():
        qrows, _ = _head_rows(q_ref[0], nh)
        gate = _nn(qrows, km_ref[...], precision=HIGHEST)
        lane = lax.broadcasted_iota(jnp.int32, gate.shape, 1)
        g = jnp.where(lane < n_blocks, gate, NEG_INF)
        lane_f = lane.astype(F32)
        out = jnp.zeros(gate.shape, F32)
        for t in range(MOBA_TOPK):
            m = jnp.max(g, axis=1, keepdims=True)
            idx = jnp.min(jnp.where(g == m, lane_f, 1e9), axis=1, keepdims=True)
            out = jnp.where(lane == t, idx, out)
            g = jnp.where(lane_f == idx, -3e38, g)
        sel_ref[0] = out[:nh].astype(jnp.int32)


def _moba_decode_kernel(pt_ref, sel_ref, q_ref, kn_ref, vn_ref, *refs, n_pages, hs):
    k_refs, v_refs, o_ref = refs[:hs * n_pages], refs[hs * n_pages:2 * hs * n_pages], refs[2 * hs * n_pages]
    qrows, own = _head_rows(q_ref[0, 0], hs)
    q16 = qrows.astype(BF16)

    def stacked(page_refs, r):
        return jnp.concatenate([page_refs[e * n_pages + r][0] for e in range(hs)], axis=0).astype(BF16)

    s_new = jnp.sum(qrows * kn_ref[0, 0], axis=1, keepdims=True)
    ss = [_nn(q16, stacked(k_refs, r)) for r in range(n_pages)]
    m = s_new
    for s in ss:
        m = jnp.maximum(m, jnp.max(s, axis=1, keepdims=True))
    p_new = jnp.exp(s_new - m)
    l = p_new
    acc = p_new * vn_ref[0, 0]
    for r, s in enumerate(ss):
        p = jnp.exp(s - m)
        l = l + jnp.sum(p, axis=1, keepdims=True)
        acc = acc + _nt(p.astype(BF16), stacked(v_refs, r))
    o_ref[0, 0] = jnp.sum(jnp.where(own, acc / l, 0.0), axis=0, keepdims=True)


def _moba_decode(pt, sel, q, kn, vn, kc, vc):
    db, ng, _, gw = q.shape
    hd = HEAD_DIM
    hs = gw // hd
    page = kc.shape[2]
    per_block = MOBA_BLOCK // page
    n_pages = MOBA_TOPK * per_block
    vec = pl.BlockSpec((1, 1, 1, gw), lambda b, g, pt, sel: (b, g, 0, 0))

    def pspec(e, r):
        t, u = divmod(r, per_block)
        return pl.BlockSpec(
            (1, hd, page),
            lambda b, g, pt, sel: (pt[b, sel[b, (g * hs + e) * MOBA_TOPK + t] * per_block + u], g * hs + e, 0))

    pages = [pspec(e, r) for e in range(hs) for r in range(n_pages)]
    return pl.pallas_call(
        functools.partial(_moba_decode_kernel, n_pages=n_pages, hs=hs),
        grid_spec=pltpu.PrefetchScalarGridSpec(
            num_scalar_prefetch=2, grid=(db, ng), in_specs=[vec, vec, vec] + pages + pages, out_specs=vec),
        out_shape=jax.ShapeDtypeStruct((db, ng, 1, gw), F32),
        compiler_params=_params(("arbitrary", "arbitrary")),
        name="moba_decode",
    )(pt, sel, q, kn, vn, *([kc] * (hs * n_pages)), *([vc] * (hs * n_pages)))


def _rope_tables(pos):
    inv = ROPE_THETA ** (-jnp.arange(HALF, dtype=F32) / HALF)
    ang = pos.astype(F32)[:, None] * inv[None, :]
    cos, sin, zero = jnp.cos(ang), jnp.sin(ang), jnp.zeros_like(ang)
    reps = LANES // HEAD_DIM
    return (jnp.tile(cos, (1, 2 * reps)), jnp.tile(jnp.concatenate([-sin, zero], axis=1), (1, reps)),
            jnp.tile(jnp.concatenate([zero, sin], axis=1), (1, reps)))


def _pick(n, candidates):
    for c in candidates:
        if n % c == 0:
            return c
    raise ValueError(f"no tile for {n}")


def kernel(x_prompt, x_sample, cache_fox_k, cache_fox_v, cache_fox_logf, cache_moba_k, cache_moba_v, state_ffn_conv, page_table, norm_mix_g, w_in, b_forget, w_fox_o, w_moba_o, w_out, norm_ffn_g, w_up, conv_w, conv_b, w_down, norm_final_g):
    batch, seq, d = x_prompt.shape
    db, dseq, _ = x_sample.shape
    depth, n_pool, page, nh, hd = cache_fox_k.shape
    aw = nh * hd
    dff = w_down.shape[1]
    n_pages = page_table.shape[1]
    past = n_pages * page
    assert depth == 1 and hd == HEAD_DIM and cache_moba_k.shape == cache_fox_k.shape and dseq == 1
    assert seq // MOBA_BLOCK <= HEAD_DIM and past // MOBA_BLOCK <= LANES
    assert seq % MOBA_BLOCK == 0 and past % MOBA_BLOCK == 0 and past // MOBA_BLOCK >= MOBA_TOPK
    assert MOBA_BLOCK % page == 0 and aw % LANES == 0 and nh <= 8

    n = batch * seq
    tm = _pick(seq, (512, 256, 128))
    t_attn = _pick(seq, (1024, 512, MOBA_BLOCK))
    tf = _pick(dff, (1408, 1024, 512, 256, 128))
    pg = _pick(n_pages, (16, 8, 4, 2))
    tables_p = _rope_tables(jnp.arange(seq))
    tables_s = _rope_tables(jnp.full((db,), past))

    xp = x_prompt.reshape(n, d)
    xs = x_sample.reshape(db, d)
    feat_major = lambda c: jnp.transpose(c, (0, 1, 3, 4, 2)).reshape(depth * n_pool, aw, page)
    fk, fv, mk, mv = (feat_major(c) for c in (cache_fox_k, cache_fox_v, cache_moba_k, cache_moba_v))
    fl = jnp.transpose(cache_fox_logf, (0, 1, 3, 2)).reshape(depth * n_pool, nh, page)
    pos_major = lambda t: jnp.transpose(t.reshape(batch, nh, -1, seq), (0, 3, 1, 2))
    sizes = (aw, aw, aw, nh, aw, aw, aw, d, d)
    offs = [sum(sizes[:i]) for i in range(len(sizes) + 1)]

    p_new, s_new = [], []
    for l in range(depth):
        cols = [w_in[l][:, offs[i]:offs[i + 1]] for i in range(len(sizes))]
        qa, ka, va, fa, qb, kb, vb, ga, gb = cols
        w_cat = jnp.concatenate([qa * ATTN_SCALE, ka, va, qb * ATTN_SCALE, kb, vb, ga, gb,
                                 jnp.pad(fa, ((0, 0), (0, LANES - nh)))], axis=1).astype(BF16)
        bf = jnp.pad(b_forget[l].astype(F32), (0, LANES - nh)).reshape(1, LANES)
        g_mix = norm_mix_g[l].reshape(1, d)
        g_ffn = norm_ffn_g[l].reshape(1, d)
        wfo, wmo, wout = w_fox_o[l].astype(BF16), w_moba_o[l].astype(BF16), w_out[l].astype(BF16)
        wup, wd = w_up[l].astype(BF16), w_down[l].astype(BF16)
        cw, cb = conv_w[l], conv_b[l].reshape(1, 2 * dff)
        gf = norm_final_g.reshape(1, d)
        pt = page_table + l * n_pool

        qa_p, ka_p, va_p, qb_p, kb_p, vb_p, ga_p, gb_p, lf_p = _inproj(
            xp, g_mix, w_cat, bf, *tables_p, tm=tm, aw=aw, nh=nh, seq=seq)
        qa_s, ka_s, va_s, qb_s, kb_s, vb_s, ga_s, gb_s, lf_s = _inproj(
            xs, g_mix, w_cat, bf, *tables_s, tm=db, aw=aw, nh=nh)
        r1 = lambda t: t.reshape(db, 1, aw)

        oa_p = _fox_prompt(qa_p, ka_p, va_p, lf_p, batch=batch, seq=seq, t=t_attn)
        ob_p, sel = _moba_prompt(pt, qb_p, kb_p, vb_p, r1(qb_s), mk, batch=batch, seq=seq, t=t_attn)
        x1_p, h2_p = _merge(oa_p, ob_p, ga_p, gb_p, xp, wfo, wmo, wout, g_ffn, tm=tm)
        xp, tails = _ffn_prompt(h2_p, x1_p, wup, cw, cb, wd, gf, seq=seq, tm=tm, sw=_pick(dff, (256, LANES)))
        tps = seq // tm
        tail_p = tails[tps - 1::tps, 8 - (CONV_W - 1):]
        p_new.append((pos_major(ka_p), pos_major(va_p), jnp.transpose(lf_p, (0, 2, 1)), pos_major(kb_p),
                      pos_major(vb_p), tail_p))

        hs = _pick(nh, (4, 2, 1))
        r3 = lambda t: t.reshape(db, nh // hs, 1, hs * hd)
        oa_s = _fox_decode(pt, r1(qa_s), r1(ka_s), r1(va_s), lf_s.reshape(db, nh, 1), fk, fv, fl, pg=pg)
        sel = sel[:, :, :MOBA_TOPK].reshape(db, nh * MOBA_TOPK)
        ob_s = _moba_decode(pt, sel, r3(qb_s), r3(kb_s), r3(vb_s), mk, mv)
        x1_s, h2_s = _merge(oa_s.reshape(db, aw), ob_s.reshape(db, aw), ga_s, gb_s, xs, wfo, wmo, wout, g_ffn, tm=db)
        prev = state_ffn_conv[l]
        xs, ua, ug = _ffn_sample(h2_s, x1_s, prev.reshape(db, (CONV_W - 1) * 2 * dff), wup, cw, cb, wd, gf, tf=tf)
        tail_s = jnp.concatenate([prev[:, 1:], jnp.concatenate([ua, ug], axis=-1)[:, None]], axis=1)
        s_new.append((ka_s.reshape(db, 1, nh, hd), va_s.reshape(db, 1, nh, hd), lf_s.reshape(db, 1, nh),
                      kb_s.reshape(db, 1, nh, hd), vb_s.reshape(db, 1, nh, hd), tail_s))

    p_out = [jnp.stack(z) for z in zip(*p_new)]
    s_out = [jnp.stack(z) for z in zip(*s_new)]
    return (xp.reshape(batch, seq, d), xs.reshape(db, 1, d), *p_out, *s_out)
```
